```python
import math
import jax, jax.numpy as jnp
from jax import lax
import numpy as np

D_MODEL = 4096
BATCH = 4
SEQ = 2048
DEPTH = 1
DEC_BATCH = 128
DEC_SEQ = 4
PAST_LEN = 2048
PAGE_SIZE = 128

N_META = 16
A_HEADS = 16
A_DK = 128
A_DV = 128
A_KW = A_HEADS * A_DK
A_VW = A_HEADS * A_DV
A_CHUNK = 64
B_HEADS = 16
B_DH = 128
B_W = B_HEADS * B_DH
Q_BLOCK = 128
SB_BIAS_INIT = -8.0
D_FF = -(-(8 * D_MODEL) // (3 * 256)) * 256
EPS = 1e-6
IN_SPLITS = [A_KW, A_KW, A_VW, A_VW, B_W, B_W, B_W, D_MODEL, D_MODEL]
IN_WIDTH = sum(IN_SPLITS)
IN_CUTS = np.cumsum(IN_SPLITS)[:-1].tolist()

kernel_name = "hgrn2_stickbreak_gated_hybrid_step"


def rmsnorm(x, g):
    xf = x.astype(jnp.float32)
    y = xf * lax.rsqrt(jnp.mean(xf * xf, axis=-1, keepdims=True) + EPS)
    return (y * g.astype(jnp.float32)).astype(x.dtype)


def mix_inputs(x, g_mix, w_in):
    h = rmsnorm(x, g_mix)
    return jnp.split(h @ w_in, IN_CUTS, axis=-1)


def hgrn_prep(qa, fa, ia, lb):
    def heads(t, d):
        return t.reshape(t.shape[0], t.shape[1], A_HEADS, d).transpose(0, 2, 1, 3).astype(jnp.float32)
    q = jax.nn.silu(heads(qa, A_DK))
    fr = heads(fa, A_DK)
    lbh = lb.reshape(A_HEADS, 1, A_DK)
    logf = jnp.log(lbh + (1.0 - lbh) * jax.nn.sigmoid(fr))
    k = (1.0 - lbh) * jax.nn.sigmoid(-fr)
    v = heads(ia, A_DV)
    return q, k, v, logf


def gla_chunk(S, q, k, v, logf):
    C = q.shape[2]
    b = jnp.cumsum(logf, axis=2)
    causal = jnp.tril(jnp.ones((C, C), dtype=bool))
    diff = b[:, :, :, None, :] - b[:, :, None, :, :]
    decay = jnp.exp(jnp.where(causal[:, :, None], diff, -jnp.inf))
    attn = jnp.einsum('nhtd,nhsd,nhtsd->nhts', q, k, decay)
    o = jnp.einsum('nhts,nhsv->nhtv', attn, v) + jnp.einsum('nhtd,nhdv->nhtv', q * jnp.exp(b), S)
    b_last = b[:, :, -1:, :]
    S_new = jnp.exp(b_last[:, :, 0, :])[..., None] * S + jnp.einsum('nhsd,nhsv->nhdv', k * jnp.exp(b_last - b), v)
    return S_new, o


def gla_scan(S0, q, k, v, logf, chunk):
    N, H, T, _ = q.shape
    n = T // chunk

    def to_chunks(t):
        return jnp.moveaxis(t.reshape(N, H, n, chunk, t.shape[-1]), 2, 0)

    def step(S, c):
        return gla_chunk(S, c[0], c[1], c[2], c[3])

    S, o = lax.scan(step, S0, (to_chunks(q), to_chunks(k), to_chunks(v), to_chunks(logf)))
    return S, jnp.moveaxis(o, 0, 2).reshape(N, H, T, o.shape[-1])


def hgrn_out(o, og, g_norm, dtype):
    N, H, T, _ = o.shape
    o = rmsnorm(o.transpose(0, 2, 1, 3), g_norm)
    g = jax.nn.silu(og.reshape(N, T, A_HEADS, A_DV).astype(jnp.float32))
    return (o * g).reshape(N, T, A_VW).astype(dtype)


def sb_prep(qb, kb, vb, gq, gk):
    N, T = qb.shape[:2]
    q = rmsnorm(qb.reshape(N, T, B_HEADS, B_DH), gq)
    k = rmsnorm(kb.reshape(N, T, B_HEADS, B_DH), gk)
    v = vb.reshape(N, T, B_HEADS, B_DH)
    return q, k, v


def sb_weights(z, mask):
    log_beta = jax.nn.log_sigmoid(z)
    log_1m = jnp.where(mask, jax.nn.log_sigmoid(-z), 0.0)
    tail = lax.cumsum(log_1m, axis=z.ndim - 1, reverse=True) - log_1m
    return jnp.where(mask, jnp.exp(log_beta + tail), 0.0)


def sb_block(qb, qpos, k, v, kpos, bias):
    scale = 1.0 / math.sqrt(B_DH)
    z = jnp.einsum('bqhd,bkhd->bhqk', qb, k, preferred_element_type=jnp.float32) * scale
    z = z + bias.astype(jnp.float32)[None, :, None, None]
    A = sb_weights(z, kpos[None, :] < qpos[:, None])
    return jnp.einsum('bhqk,bkhd->bqhd', A, v.astype(jnp.float32))


def sb_prompt(q, k, v, bias):
    N, T = q.shape[:2]
    pos = jnp.arange(T)
    o_meta = sb_block(q[:, :N_META], pos[:N_META], k[:, :N_META], v[:, :N_META], pos[:N_META], bias)
    n_blk = (T - N_META) // Q_BLOCK
    qr = q[:, N_META:].reshape(N, n_blk, Q_BLOCK, B_HEADS, B_DH).transpose(1, 0, 2, 3, 4)
    pr = pos[N_META:].reshape(n_blk, Q_BLOCK)
    o_r = lax.map(lambda a: sb_block(a[0], a[1], k, v, pos, bias), (qr, pr))
    o_r = o_r.transpose(1, 0, 2, 3, 4).reshape(N, T - N_META, B_HEADS, B_DH)
    return jnp.concatenate([o_meta, o_r], axis=1)


def sb_sample(q, k_new, v_new, k_past, v_past, bias):
    scale = 1.0 / math.sqrt(B_DH)
    Tn, P = q.shape[1], k_past.shape[1]
    z = jnp.concatenate([
        jnp.einsum('bqhd,bkhd->bhqk', q, k_past, preferred_element_type=jnp.float32),
        jnp.einsum('bqhd,bkhd->bhqk', q, k_new, preferred_element_type=jnp.float32)], axis=-1) * scale
    z = z + bias.astype(jnp.float32)[None, :, None, None]
    mask = jnp.concatenate([jnp.ones((Tn, P), dtype=bool), jnp.tril(jnp.ones((Tn, Tn), dtype=bool), -1)], axis=1)
    A = sb_weights(z, mask)
    return (jnp.einsum('bhqk,bkhd->bqhd', A[..., :P], v_past.astype(jnp.float32))
            + jnp.einsum('bhqk,bkhd->bqhd', A[..., P:], v_new.astype(jnp.float32)))


def finish_layer(x, oa, ob, gta, gtb, w_pa, w_pb, w_o, g_ffn, w_gate, w_up, w_down):
    merged = jax.nn.sigmoid(gta) * (oa @ w_pa) + jax.nn.sigmoid(gtb) * (ob @ w_pb)
    x = x + merged @ w_o
    h = rmsnorm(x, g_ffn)
    return x + (jax.nn.silu(h @ w_gate) * (h @ w_up)) @ w_down


def setup_inputs(seed: int = 0) -> dict:
    key = jax.random.key(seed)
    ks = jax.random.split(key, 24)
    n_pages = PAST_LEN // PAGE_SIZE
    n_used = DEC_BATCH * n_pages
    n_phys = n_used + n_used // 4

    def nrm(k, shape, s):
        return jax.random.normal(k, shape, jnp.float32) * s

    page_table = jax.random.permutation(ks[5], n_phys)[:n_used].reshape(DEC_BATCH, n_pages).astype(jnp.int32)
    return {
        "x_prompt": nrm(ks[0], (BATCH, SEQ, D_MODEL), 1.0),
        "x_sample": nrm(ks[1], (DEC_BATCH, DEC_SEQ, D_MODEL), 1.0),
        "cache_k": nrm(ks[2], (DEPTH, n_phys, PAGE_SIZE, B_HEADS, B_DH), 1.0),
        "cache_v": nrm(ks[3], (DEPTH, n_phys, PAGE_SIZE, B_HEADS, B_DH), 1.0),
        "state_hgrn": nrm(ks[4], (DEPTH, DEC_BATCH, A_HEADS, A_DK, A_DV), 0.5),
        "page_table": page_table,
        "meta_tokens": nrm(ks[6], (N_META, D_MODEL), 1.0),
        "g_mix": 1.0 + nrm(ks[7], (DEPTH, D_MODEL), 0.02),
        "w_in": nrm(ks[8], (DEPTH, D_MODEL, IN_WIDTH), D_MODEL ** -0.5),
        "hgrn_lb_logits": nrm(ks[9], (DEPTH + 1, A_KW), 0.5),
        "hgrn_g_norm": 1.0 + nrm(ks[10], (DEPTH, A_DV), 0.02),
        "sb_gq": 1.0 + nrm(ks[11], (DEPTH, B_DH), 0.02),
        "sb_gk": 1.0 + nrm(ks[12], (DEPTH, B_DH), 0.02),
        "sb_bias": SB_BIAS_INIT + nrm(ks[20], (DEPTH, B_HEADS), 0.1),
        "w_pa": nrm(ks[13], (DEPTH, A_VW, D_MODEL), A_VW ** -0.5),
        "w_pb": nrm(ks[14], (DEPTH, B_W, D_MODEL), B_W ** -0.5),
        "w_o": nrm(ks[15], (DEPTH, D_MODEL, D_MODEL), D_MODEL ** -0.5),
        "g_ffn": 1.0 + nrm(ks[16], (DEPTH, D_MODEL), 0.02),
        "w_gate": nrm(ks[17], (DEPTH, D_MODEL, D_FF), D_MODEL ** -0.5),
        "w_up": nrm(ks[18], (DEPTH, D_MODEL, D_FF), D_MODEL ** -0.5),
        "w_down": nrm(ks[19], (DEPTH, D_FF, D_MODEL), D_FF ** -0.5),
    }


def reference(x_prompt, x_sample, cache_k, cache_v, state_hgrn, page_table, meta_tokens, g_mix, w_in,
              hgrn_lb_logits, hgrn_g_norm, sb_gq, sb_gk, sb_bias, w_pa, w_pb, w_o, g_ffn, w_gate, w_up, w_down):
    dt = x_prompt.dtype
    lb_all = jnp.cumsum(jax.nn.softmax(hgrn_lb_logits.astype(jnp.float32), axis=0), axis=0)
    nb = x_prompt.shape[0]
    xp = jnp.concatenate([jnp.broadcast_to(meta_tokens.astype(dt)[None], (nb, N_META, D_MODEL)), x_prompt], axis=1)
    xs = x_sample
    db = xs.shape[0]
    kp_l, vp_l, sp_l, ks_l, vs_l, ss_l = [], [], [], [], [], []
    for l in range(DEPTH):
        lw = (w_pa[l], w_pb[l], w_o[l], g_ffn[l], w_gate[l], w_up[l], w_down[l])
        qa, fa, ia, oga, qb, kb, vb, gta, gtb = mix_inputs(xp, g_mix[l], w_in[l])
        q, k, v, logf = hgrn_prep(qa, fa, ia, lb_all[l])
        S0 = jnp.zeros((nb, A_HEADS, A_DK, A_DV), jnp.float32)
        S_meta, o_meta = gla_chunk(S0, q[:, :, :N_META], k[:, :, :N_META], v[:, :, :N_META], logf[:, :, :N_META])
        S_p, o_real = gla_scan(S_meta, q[:, :, N_META:], k[:, :, N_META:], v[:, :, N_META:], logf[:, :, N_META:], A_CHUNK)
        oa = hgrn_out(jnp.concatenate([o_meta, o_real], axis=2), oga, hgrn_g_norm[l], dt)
        sq, sk, sv = sb_prep(qb, kb, vb, sb_gq[l], sb_gk[l])
        ob = sb_prompt(sq, sk, sv, sb_bias[l]).astype(dt).reshape(nb, xp.shape[1], B_W)
        xp = finish_layer(xp, oa, ob, gta, gtb, *lw)
        kp_l.append(sk)
        vp_l.append(sv)
        sp_l.append(S_p.astype(dt))
        qa, fa, ia, oga, qb, kb, vb, gta, gtb = mix_inputs(xs, g_mix[l], w_in[l])
        q, k, v, logf = hgrn_prep(qa, fa, ia, lb_all[l])
        S_s, o_s = gla_chunk(state_hgrn[l].astype(jnp.float32), q, k, v, logf)
        oa = hgrn_out(o_s, oga, hgrn_g_norm[l], xs.dtype)
        sq, sk, sv = sb_prep(qb, kb, vb, sb_gq[l], sb_gk[l])
        k_past = cache_k[l][page_table].reshape(db, -1, B_HEADS, B_DH)
        v_past = cache_v[l][page_table].reshape(db, -1, B_HEADS, B_DH)
        ob = sb_sample(sq, sk, sv, k_past, v_past, sb_bias[l]).astype(xs.dtype).reshape(db, xs.shape[1], B_W)
        xs = finish_layer(xs, oa, ob, gta, gtb, *lw)
        ks_l.append(sk.astype(cache_k.dtype))
        vs_l.append(sv.astype(cache_v.dtype))
        ss_l.append(S_s.astype(state_hgrn.dtype))
    y_prompt = xp[:, N_META:]
    return (y_prompt, xs, jnp.stack(kp_l), jnp.stack(vp_l), jnp.stack(sp_l),
            jnp.stack(ks_l), jnp.stack(vs_l), jnp.stack(ss_l))
```

```python
import functools
import math

import jax
import jax.numpy as jnp
from jax import lax
from jax.experimental import pallas as pl
from jax.experimental.pallas import tpu as pltpu

N_META = 16
HEADS = 16
DH = 128
HW = HEADS * DH
EPS = 1e-6
GLA_CHUNK = 16
SB_BLOCK = 128
EXP_CLAMP = 80.0
VMEM_LIMIT = 56 * 1024 * 1024

COL_QA, COL_FA, COL_IA, COL_OG, COL_QB, COL_KB, COL_VB = range(7)
COL_GTA = 7 * HW

_NT = (((1,), (1,)), ((), ()))
_TN = (((0,), (0,)), ((), ()))
BF = jnp.bfloat16
F32 = jnp.float32


def _params(sem):
    return pltpu.CompilerParams(dimension_semantics=sem, vmem_limit_bytes=VMEM_LIMIT)


def _sigmoid(x):
    return 1.0 / (1.0 + jnp.exp(-x))


def _row_rmsnorm(x, g):
    return x * lax.rsqrt(jnp.mean(x * x, axis=-1, keepdims=True) + EPS) * g


def _rmsnorm_kernel(x_ref, g_ref, o_ref):
    o_ref[...] = _row_rmsnorm(x_ref[...], g_ref[...]).astype(o_ref.dtype)


def _rmsnorm_bf16(x, g, tm=256):
    m, d = x.shape
    tm = min(tm, m)
    return pl.pallas_call(
        _rmsnorm_kernel,
        grid=(pl.cdiv(m, tm),),
        in_specs=[pl.BlockSpec((tm, d), lambda i: (i, 0)), pl.BlockSpec((1, d), lambda i: (0, 0))],
        out_specs=pl.BlockSpec((tm, d), lambda i: (i, 0)),
        out_shape=jax.ShapeDtypeStruct((m, d), BF),
        compiler_params=_params(("parallel",)),
        name="rmsnorm_bf16",
    )(x, g.reshape(1, d))


def _dense_kernel(*refs, mode):
    o_ref = refs[-1]
    if mode == "plain":
        a, w = refs[:2]
        o_ref[...] = jnp.dot(a[...], w[...], preferred_element_type=F32).astype(o_ref.dtype)
    elif mode == "residual":
        a, w, r = refs[:3]
        o_ref[...] = r[...] + jnp.dot(a[...], w[...], preferred_element_type=F32)
    elif mode == "swiglu":
        a, wg, wu = refs[:3]
        g = jnp.dot(a[...], wg[...], preferred_element_type=F32)
        u = jnp.dot(a[...], wu[...], preferred_element_type=F32)
        o_ref[...] = (g * _sigmoid(g) * u).astype(o_ref.dtype)
    elif mode == "gated_sum":
        a, b, wa, wb, ga, gb = refs[:6]
        pa = jnp.dot(a[...], wa[...], preferred_element_type=F32)
        pb = jnp.dot(b[...], wb[...], preferred_element_type=F32)
        o_ref[...] = (_sigmoid(ga[...]) * pa + _sigmoid(gb[...]) * pb).astype(o_ref.dtype)
    else:
        raise ValueError(mode)


def _dense(mode, acts, weights, extras, extra_col_offsets, out_dtype, tm, tn, name):
    m = acts[0].shape[0]
    n = weights[0].shape[1]
    tm = min(tm, m)
    tn = min(tn, n)
    in_specs = [pl.BlockSpec((tm, a.shape[1]), lambda i, j: (i, 0)) for a in acts]
    in_specs += [pl.BlockSpec((w.shape[0], tn), lambda i, j: (0, j)) for w in weights]
    for off in extra_col_offsets:
        assert off % tn == 0
        in_specs.append(pl.BlockSpec((tm, tn), lambda i, j, o=off // tn: (i, j + o)))
    return pl.pallas_call(
        functools.partial(_dense_kernel, mode=mode),
        grid=(pl.cdiv(m, tm), pl.cdiv(n, tn)),
        in_specs=in_specs,
        out_specs=pl.BlockSpec((tm, tn), lambda i, j: (i, j)),
        out_shape=jax.ShapeDtypeStruct((m, n), out_dtype),
        compiler_params=_params(("parallel", "arbitrary")),
        name=name,
    )(*acts, *weights, *extras)


def _split3(x):
    hi = x.astype(BF)
    r = x - hi.astype(F32)
    mid = r.astype(BF)
    lo = (r - mid.astype(F32)).astype(BF)
    return hi, mid, lo


def _gla_chunk(st, qa, fr, ia, lb, tri, valid=None):
    c = qa.shape[0]
    e = jnp.exp(-jnp.abs(fr))
    inv = 1.0 / (1.0 + e)
    pos = fr >= 0
    sig = jnp.where(pos, inv, e * inv)
    nsig = jnp.where(pos, e * inv, inv)
    logf = jnp.log(lb + (1.0 - lb) * sig)
    k = (1.0 - lb) * nsig
    if valid is not None:
        logf = jnp.where(valid, logf, 0.0)
        k = jnp.where(valid, k, 0.0)
    q = qa * _sigmoid(qa)
    v = ia.astype(BF)
    hi, mid, lo = _split3(logf)
    b = (jnp.dot(tri, hi, preferred_element_type=F32) + jnp.dot(tri, mid, preferred_element_type=F32)
         + jnp.dot(tri, lo, preferred_element_type=F32))
    r = c // 2 - 1
    bm = b[r:r + 1, :]
    qh = (q * jnp.exp(jnp.minimum(b - bm, EXP_CLAMP))).astype(BF)
    kh = (k * jnp.exp(jnp.minimum(bm - b, EXP_CLAMP))).astype(BF)
    attn = lax.dot_general(qh, kh, _NT, preferred_element_type=F32)
    row = lax.broadcasted_iota(jnp.int32, (c, c), 0)
    col = lax.broadcasted_iota(jnp.int32, (c, c), 1)
    attn = jnp.where(row >= col, attn, 0.0)
    qs = (q * jnp.exp(b)).astype(BF)
    o = (jnp.dot(attn.astype(BF), v, preferred_element_type=F32)
         + lax.dot_general(qs, st.astype(BF), _NT, preferred_element_type=F32))
    bl = b[c - 1:c, :]
    kt = (k * jnp.exp(bl - b)).astype(BF)
    st_new = st * jnp.exp(bl) + lax.dot_general(v, kt, _TN, preferred_element_type=F32)
    return st_new, o


def _gla_out(o, og, gn):
    return (_row_rmsnorm(o, gn) * (og * _sigmoid(og))).astype(BF)


def _tri(c):
    row = lax.broadcasted_iota(jnp.int32, (c, c), 0)
    col = lax.broadcasted_iota(jnp.int32, (c, c), 1)
    return (row >= col).astype(BF)


def _gla_prompt_kernel(qa_ref, fa_ref, ia_ref, og_ref, mfa_ref, mia_ref, lb_ref, gn_ref,
                       oa_ref, s_ref, *, n_chunks):
    lb = lb_ref[...]
    gn = gn_ref[...]
    tri = _tri(GLA_CHUNK)
    c = GLA_CHUNK
    st = jnp.zeros((DH, DH), F32)
    for m0 in range(0, N_META, c):
        st, _ = _gla_chunk(st, jnp.zeros((c, DH), F32), mfa_ref[m0:m0 + c, :], mia_ref[m0:m0 + c, :], lb, tri)

    def body(i, st):
        rows = pl.ds(pl.multiple_of(i * c, c), c)
        st, o = _gla_chunk(st, qa_ref[rows, :], fa_ref[rows, :], ia_ref[rows, :], lb, tri)
        oa_ref[rows, :] = _gla_out(o, og_ref[rows, :], gn)
        return st

    st = lax.fori_loop(0, n_chunks, body, st, unroll=2)
    s_ref[...] = st.T


def _gla_prompt(proj, proj_meta, lb, g_norm, nb, t):
    blk = lambda cb: pl.BlockSpec((t, DH), lambda n, h, cb=cb: (n, cb * HEADS + h))
    mblk = lambda cb: pl.BlockSpec((N_META, DH), lambda n, h, cb=cb: (0, cb * HEADS + h))
    return pl.pallas_call(
        functools.partial(_gla_prompt_kernel, n_chunks=t // GLA_CHUNK),
        grid=(nb, HEADS),
        in_specs=[blk(COL_QA), blk(COL_FA), blk(COL_IA), blk(COL_OG), mblk(COL_FA), mblk(COL_IA),
                  pl.BlockSpec((1, DH), lambda n, h: (0, h)), pl.BlockSpec((1, DH), lambda n, h: (0, 0))],
        out_specs=[pl.BlockSpec((t, DH), lambda n, h: (n, h)),
                   pl.BlockSpec((None, None, DH, DH), lambda n, h: (n, h, 0, 0))],
        out_shape=[jax.ShapeDtypeStruct((nb * t, HW), BF),
                   jax.ShapeDtypeStruct((nb, HEADS, DH, DH), F32)],
        compiler_params=_params(("parallel", "parallel")),
        name="gla_prompt",
    )(proj, proj, proj, proj, proj_meta, proj_meta, lb.reshape(1, HW), g_norm.reshape(1, DH))


def _gla_sample_kernel(qa_ref, fa_ref, ia_ref, og_ref, s0_ref, lb_ref, gn_ref, oa_ref, s_ref, *, tn):
    c = 8
    tri = _tri(c)
    gn = gn_ref[...]
    valid = lax.broadcasted_iota(jnp.int32, (c, DH), 0) < tn
    pad = jnp.zeros((c - tn, DH), F32)
    for h in range(HEADS):
        cols = slice(h * DH, (h + 1) * DH)
        qa = jnp.concatenate([qa_ref[:, cols], pad], axis=0)
        fr = jnp.concatenate([fa_ref[:, cols], pad], axis=0)
        ia = jnp.concatenate([ia_ref[:, cols], pad], axis=0)
        og = jnp.concatenate([og_ref[:, cols], pad], axis=0)
        st, o = _gla_chunk(s0_ref[h].T, qa, fr, ia, lb_ref[:, cols], tri, valid)
        oa_ref[:, cols] = _gla_out(o, og, gn)[:tn, :]
        s_ref[h] = st.T


def _gla_sample(proj3, state, lb, g_norm):
    db, tn, _ = proj3.shape
    blk = lambda cb: pl.BlockSpec((None, tn, HW), lambda b, cb=cb: (b, 0, cb))
    sblk = pl.BlockSpec((None, HEADS, DH, DH), lambda b: (b, 0, 0, 0))
    return pl.pallas_call(
        functools.partial(_gla_sample_kernel, tn=tn),
        grid=(db,),
        in_specs=[blk(COL_QA), blk(COL_FA), blk(COL_IA), blk(COL_OG), sblk,
                  pl.BlockSpec((1, HW), lambda b: (0, 0)), pl.BlockSpec((1, DH), lambda b: (0, 0))],
        out_specs=[pl.BlockSpec((None, tn, HW), lambda b: (b, 0, 0)), sblk],
        out_shape=[jax.ShapeDtypeStruct((db, tn, HW), BF),
                   jax.ShapeDtypeStruct((db, HEADS, DH, DH), F32)],
        compiler_params=_params(("parallel",)),
        name="gla_sample",
    )(proj3, proj3, proj3, proj3, state, lb.reshape(1, HW), g_norm.reshape(1, DH))


def _sb_block(z, v, c, acc, w, mask):
    kk = z.shape[1]
    log_beta = jnp.minimum(z, 0.0) - jnp.log1p(jnp.exp(-jnp.abs(z)))
    l1m = log_beta - z
    if mask is not None:
        l1m = jnp.where(mask, l1m, 0.0)
    hi = l1m.astype(BF)
    lo = (l1m - hi.astype(F32)).astype(BF)
    r = jnp.dot(hi, w, preferred_element_type=F32) + jnp.dot(lo, w, preferred_element_type=F32)
    a = jnp.exp(log_beta + r[:, :kk] + c)
    if mask is not None:
        a = jnp.where(mask, a, 0.0)
    acc = acc + jnp.dot(a.astype(BF), v, preferred_element_type=F32)
    return c + r[:, kk:], acc


def _sb_w(kk):
    row = lax.broadcasted_iota(jnp.int32, (kk, 2 * kk), 0)
    col = lax.broadcasted_iota(jnp.int32, (kk, 2 * kk), 1)
    return jnp.logical_or(col >= kk, row > col).astype(BF)


def _sb_prompt_kernel(qb_ref, kb_ref, vb_ref, mk_ref, mv_ref, gq_ref, gk_ref, bias_ref,
                      ob_ref, ko_ref, vo_ref, q_s, k_s, v_s, mk_s, mv_s, w_s):
    i = pl.program_id(2)
    blk = SB_BLOCK
    scale = 1.0 / math.sqrt(DH)

    @pl.when(i == 0)
    def _():
        kn = _row_rmsnorm(kb_ref[...], gk_ref[...])
        ko_ref[N_META:, :] = kn
        k_s[...] = kn.astype(BF)
        mkn = _row_rmsnorm(mk_ref[...], gk_ref[...])
        ko_ref[:N_META, :] = mkn
        mk_s[...] = jnp.zeros(mk_s.shape, BF)
        mk_s[:N_META, :] = mkn.astype(BF)
        v = vb_ref[...]
        vo_ref[N_META:, :] = v
        v_s[...] = v.astype(BF)
        mv = mv_ref[...]
        vo_ref[:N_META, :] = mv
        mv_s[...] = jnp.zeros(mv_s.shape, BF)
        mv_s[:N_META, :] = mv.astype(BF)
        q_s[...] = _row_rmsnorm(qb_ref[...], gq_ref[...]).astype(BF)
        w_s[...] = _sb_w(blk)

    bias = bias_ref[...]
    w = w_s[...]
    qi = q_s[pl.ds(pl.multiple_of(i * blk, blk), blk), :]
    row = lax.broadcasted_iota(jnp.int32, (blk, blk), 0)
    col = lax.broadcasted_iota(jnp.int32, (blk, blk), 1)

    def logits(kj):
        return lax.dot_general(qi, kj, _NT, preferred_element_type=F32) * scale + bias

    def rows(j):
        return pl.ds(pl.multiple_of(j * blk, blk), blk)

    c, acc = _sb_block(logits(k_s[rows(i), :]), v_s[rows(i), :], jnp.zeros((blk, blk), F32),
                       jnp.zeros((blk, DH), F32), w, col < row)

    def body(jj, carry):
        j = i - 1 - jj
        return _sb_block(logits(k_s[rows(j), :]), v_s[rows(j), :], carry[0], carry[1], w, None)

    c, acc = lax.fori_loop(0, i, body, (c, acc))
    c, acc = _sb_block(logits(mk_s[...]), mv_s[...], c, acc, w, col < N_META)
    ob_ref[...] = acc.astype(ob_ref.dtype)


def _sb_prompt(proj, proj_meta, gq, gk, bias, nb, t):
    nq = t // SB_BLOCK
    blk = lambda cb: pl.BlockSpec((t, DH), lambda n, h, i, cb=cb: (n, cb * HEADS + h))
    mblk = lambda cb: pl.BlockSpec((N_META, DH), lambda n, h, i, cb=cb: (0, cb * HEADS + h))
    vec = pl.BlockSpec((1, DH), lambda n, h, i: (0, 0))
    kv_out = pl.BlockSpec((None, N_META + t, DH), lambda n, h, i: (n, 0, h))
    bias_b = jnp.broadcast_to(bias.astype(F32).reshape(HEADS, 1, 1), (HEADS, 1, DH))
    return pl.pallas_call(
        _sb_prompt_kernel,
        grid=(nb, HEADS, nq),
        in_specs=[blk(COL_QB), blk(COL_KB), blk(COL_VB), mblk(COL_KB), mblk(COL_VB), vec, vec,
                  pl.BlockSpec((None, 1, DH), lambda n, h, i: (h, 0, 0))],
        out_specs=[pl.BlockSpec((SB_BLOCK, DH), lambda n, h, i: (n * nq + i, h)), kv_out, kv_out],
        out_shape=[jax.ShapeDtypeStruct((nb * t, HW), BF),
                   jax.ShapeDtypeStruct((nb, N_META + t, HW), F32),
                   jax.ShapeDtypeStruct((nb, N_META + t, HW), F32)],
        scratch_shapes=[pltpu.VMEM((t, DH), BF), pltpu.VMEM((t, DH), BF), pltpu.VMEM((t, DH), BF),
                        pltpu.VMEM((SB_BLOCK, DH), BF), pltpu.VMEM((SB_BLOCK, DH), BF),
                        pltpu.VMEM((SB_BLOCK, 2 * SB_BLOCK), BF)],
        compiler_params=_params(("parallel", "parallel", "arbitrary")),
        name="sb_prompt",
    )(proj, proj, proj, proj_meta, proj_meta, gq.reshape(1, DH), gk.reshape(1, DH), bias_b)


def _sb_sample_kernel(pt_ref, qb_ref, kb_ref, vb_ref, ck_ref, cv_ref, gq_ref, gk_ref, bias_ref,
                      ob_ref, ko_ref, vo_ref, q_s, c_s, acc_s, w_s, *, tn, n_pages):
    p = pl.program_id(1)
    rq = 16
    pg = ck_ref.shape[0]
    scale = 1.0 / math.sqrt(DH)

    def sweep(get_k, get_v, mask):
        w = w_s[...]
        for h in range(HEADS):
            rows = slice(h * rq, (h + 1) * rq)
            z = lax.dot_general(q_s[rows, :], get_k(h), _NT, preferred_element_type=F32) * scale + bias_ref[h]
            c, acc = _sb_block(z, get_v(h), c_s[rows, :], acc_s[rows, :], w, mask)
            c_s[rows, :] = c
            acc_s[rows, :] = acc

    @pl.when(p == 0)
    def _():
        w_s[...] = _sb_w(pg)
        c_s[...] = jnp.zeros(c_s.shape, F32)
        acc_s[...] = jnp.zeros(acc_s.shape, F32)
        q_s[...] = jnp.zeros(q_s.shape, BF)
        vo_ref[...] = vb_ref[...]
        kpad = jnp.zeros((pg - tn, DH), BF)
        ks, vs = [], []
        for h in range(HEADS):
            cols = slice(h * DH, (h + 1) * DH)
            q_s[h * rq:h * rq + tn, :] = _row_rmsnorm(qb_ref[:, cols], gq_ref[...]).astype(BF)
            kn = _row_rmsnorm(kb_ref[:, cols], gk_ref[...])
            ko_ref[:, cols] = kn
            ks.append(jnp.concatenate([kn.astype(BF), kpad], axis=0))
            vs.append(jnp.concatenate([vb_ref[:, cols].astype(BF), kpad], axis=0))
        row = lax.broadcasted_iota(jnp.int32, (rq, pg), 0)
        col = lax.broadcasted_iota(jnp.int32, (rq, pg), 1)
        sweep(lambda h: ks[h], lambda h: vs[h], jnp.logical_and(col < row, col < tn))

    sweep(lambda h: ck_ref[:, h, :].astype(BF), lambda h: cv_ref[:, h, :].astype(BF), None)

    @pl.when(p == n_pages - 1)
    def _():
        for h in range(HEADS):
            ob_ref[:, h * DH:(h + 1) * DH] = acc_s[h * rq:h * rq + tn, :].astype(ob_ref.dtype)


def _sb_sample(proj3, cache_k, cache_v, page_table, gq, gk, bias):
    db, tn, _ = proj3.shape
    n_pages = page_table.shape[1]
    pg = cache_k.shape[1]
    blk = lambda cb: pl.BlockSpec((None, tn, HW), lambda b, p, pt, cb=cb: (b, 0, cb))
    page = pl.BlockSpec((None, pg, HEADS, DH), lambda b, p, pt: (pt[b, n_pages - 1 - p], 0, 0, 0))
    vec = pl.BlockSpec((1, DH), lambda b, p, pt: (0, 0))
    out = pl.BlockSpec((None, tn, HW), lambda b, p, pt: (b, 0, 0))
    bias_b = jnp.broadcast_to(bias.astype(F32).reshape(HEADS, 1, 1), (HEADS, 1, pg))
    return pl.pallas_call(
        functools.partial(_sb_sample_kernel, tn=tn, n_pages=n_pages),
        grid_spec=pltpu.PrefetchScalarGridSpec(
            num_scalar_prefetch=1,
            grid=(db, n_pages),
            in_specs=[blk(COL_QB), blk(COL_KB), blk(COL_VB), page, page, vec, vec,
                      pl.BlockSpec((HEADS, 1, pg), lambda b, p, pt: (0, 0, 0))],
            out_specs=[out, out, out],
            scratch_shapes=[pltpu.VMEM((HEADS * 16, DH), BF), pltpu.VMEM((HEADS * 16, pg), F32),
                            pltpu.VMEM((HEADS * 16, DH), F32), pltpu.VMEM((pg, 2 * pg), BF)]),
        out_shape=[jax.ShapeDtypeStruct((db, tn, HW), BF),
                   jax.ShapeDtypeStruct((db, tn, HW), F32),
                   jax.ShapeDtypeStruct((db, tn, HW), F32)],
        compiler_params=_params(("parallel", "arbitrary")),
        name="sb_sample",
    )(page_table, proj3, proj3, proj3, cache_k, cache_v, gq.reshape(1, DH), gk.reshape(1, DH), bias_b)


def _finish(x, oa, ob, proj, w_pa, w_pb, w_o, g_ffn, w_gate, w_up, w_down, tm):
    d = x.shape[1]
    merged = _dense("gated_sum", [oa, ob], [w_pa, w_pb], [proj, proj], [COL_GTA, COL_GTA + d], BF,
                    tm, 512, "merge_proj")
    x1 = _dense("residual", [merged], [w_o], [x], [0], F32, tm, 512, "out_proj")
    h = _rmsnorm_bf16(x1, g_ffn)
    act = _dense("swiglu", [h], [w_gate, w_up], [], [], BF, tm, 512, "ffn_up")
    return _dense("residual", [act], [w_down], [x1], [0], F32, min(tm, 512), 512, "ffn_down")


def kernel(x_prompt, x_sample, cache_k, cache_v, state_hgrn, page_table, meta_tokens, g_mix, w_in,
           hgrn_lb_logits, hgrn_g_norm, sb_gq, sb_gk, sb_bias, w_pa, w_pb, w_o, g_ffn, w_gate, w_up, w_down):
    nb, t, d = x_prompt.shape
    db, tn, _ = x_sample.shape
    depth = w_in.shape[0]
    assert depth == 1, "single-layer trunk"
    assert t % SB_BLOCK == 0 and t % GLA_CHUNK == 0 and N_META % GLA_CHUNK == 0
    l = 0
    lb = jnp.cumsum(jax.nn.softmax(hgrn_lb_logits.astype(F32), axis=0), axis=0)[l]
    lead = lambda a: a.reshape(a.shape[1:])
    wb = lambda w: lead(w).astype(BF)
    w_in_b = wb(w_in)
    lw = (wb(w_pa), wb(w_pb), wb(w_o), g_ffn[l], wb(w_gate), wb(w_up), wb(w_down))

    xp = x_prompt.reshape(nb * t, d)
    xs = x_sample.reshape(db * tn, d)

    def in_proj(x, tm):
        return _dense("plain", [_rmsnorm_bf16(x, g_mix[l])], [w_in_b], [], [], F32, tm, 1024, "in_proj")

    proj_m = in_proj(meta_tokens.astype(x_prompt.dtype), N_META)
    proj_p = in_proj(xp, 1024)
    proj_s = in_proj(xs, 512)
    proj_s3 = proj_s.reshape(db, tn, -1)

    oa_p, s_p = _gla_prompt(proj_p, proj_m, lb, hgrn_g_norm[l], nb, t)
    ob_p, k_p, v_p = _sb_prompt(proj_p, proj_m, sb_gq[l], sb_gk[l], sb_bias[l], nb, t)
    y_p = _finish(xp, oa_p, ob_p, proj_p, *lw, tm=1024)

    oa_s, s_s = _gla_sample(proj_s3, lead(state_hgrn), lb, hgrn_g_norm[l])
    ob_s, k_s, v_s = _sb_sample(proj_s3, lead(cache_k), lead(cache_v), page_table, sb_gq[l], sb_gk[l],
                                sb_bias[l])
    y_s = _finish(xs, oa_s.reshape(db * tn, HW), ob_s.reshape(db * tn, HW), proj_s, *lw, tm=512)

    return (y_p.reshape(nb, t, d), y_s.reshape(db, tn, d),
            k_p.reshape(1, nb, N_META + t, HEADS, DH), v_p.reshape(1, nb, N_META + t, HEADS, DH),
            s_p[None],
            k_s.reshape(1, db, tn, HEADS, DH).astype(cache_k.dtype),
            v_s.reshape(1, db, tn, HEADS, DH).astype(cache_v.dtype),
            s_s[None].astype(state_hgrn.dtype))
```

```python
import functools
import math

import jax
import jax.numpy as jnp
from jax import lax
from jax.experimental import pallas as pl
from jax.experimental.pallas import tpu as pltpu

N_META = 16
HEADS = 16
DH = 128
HW = HEADS * DH
EPS = 1e-6
SUBLANES = 8
GLA_CHUNK = 16
GLA_ROWS = 128
SB_BLOCK = 128
SB_QTILE = 512
SB_PAGES_PER_STEP = 4
EXP_CLAMP = 80.0
VMEM_LIMIT = 56 * 1024 * 1024

COL_QA, COL_FA, COL_IA, COL_OG, COL_QB, COL_KB, COL_VB = range(7)
COL_GTA = 7 * HW

_NT = (((1,), (1,)), ((), ()))
_TN = (((0,), (0,)), ((), ()))
BF = jnp.bfloat16
F32 = jnp.float32


def _params(sem):
    return pltpu.CompilerParams(dimension_semantics=sem, vmem_limit_bytes=VMEM_LIMIT)


def _sigmoid(x):
    return 1.0 / (1.0 + jnp.exp(-x))


def _row_rmsnorm(x, g):
    return x * lax.rsqrt(jnp.mean(x * x, axis=-1, keepdims=True) + EPS) * g


def _rmsnorm_kernel(x_ref, g_ref, o_ref):
    o_ref[...] = _row_rmsnorm(x_ref[...], g_ref[...]).astype(o_ref.dtype)


def _rmsnorm_bf16(x, g, tm=256):
    m, d = x.shape
    tm = min(tm, m)
    return pl.pallas_call(
        _rmsnorm_kernel,
        grid=(pl.cdiv(m, tm),),
        in_specs=[pl.BlockSpec((tm, d), lambda i: (i, 0)), pl.BlockSpec((1, d), lambda i: (0, 0))],
        out_specs=pl.BlockSpec((tm, d), lambda i: (i, 0)),
        out_shape=jax.ShapeDtypeStruct((m, d), BF),
        compiler_params=_params(("parallel",)),
        name="rmsnorm_bf16",
    )(x, g.reshape(1, d))


def _dense_kernel(*refs, mode):
    o_ref = refs[-1]
    if mode == "plain":
        a, w = refs[:2]
        o_ref[...] = jnp.dot(a[...], w[...], preferred_element_type=F32).astype(o_ref.dtype)
    elif mode == "residual":
        a, w, r = refs[:3]
        o_ref[...] = r[...] + jnp.dot(a[...], w[...], preferred_element_type=F32)
    elif mode == "swiglu":
        a, wg, wu = refs[:3]
        g = jnp.dot(a[...], wg[...], preferred_element_type=F32)
        u = jnp.dot(a[...], wu[...], preferred_element_type=F32)
        o_ref[...] = (g * _sigmoid(g) * u).astype(o_ref.dtype)
    elif mode == "gated_sum":
        a, b, wa, wb, ga, gb = refs[:6]
        pa = jnp.dot(a[...], wa[...], preferred_element_type=F32)
        pb = jnp.dot(b[...], wb[...], preferred_element_type=F32)
        o_ref[...] = (_sigmoid(ga[...]) * pa + _sigmoid(gb[...]) * pb).astype(o_ref.dtype)
    else:
        raise ValueError(mode)


def _dense(mode, acts, weights, extras, extra_col_offsets, out_dtype, tm, tn, name):
    m = acts[0].shape[0]
    n = weights[0].shape[1]
    tm = min(tm, m)
    tn = min(tn, n)
    in_specs = [pl.BlockSpec((tm, a.shape[1]), lambda i, j: (i, 0)) for a in acts]
    in_specs += [pl.BlockSpec((w.shape[0], tn), lambda i, j: (0, j)) for w in weights]
    for off in extra_col_offsets:
        assert off % tn == 0
        in_specs.append(pl.BlockSpec((tm, tn), lambda i, j, o=off // tn: (i, j + o)))
    return pl.pallas_call(
        functools.partial(_dense_kernel, mode=mode),
        grid=(pl.cdiv(m, tm), pl.cdiv(n, tn)),
        in_specs=in_specs,
        out_specs=pl.BlockSpec((tm, tn), lambda i, j: (i, j)),
        out_shape=jax.ShapeDtypeStruct((m, n), out_dtype),
        compiler_params=_params(("parallel", "arbitrary")),
        name=name,
    )(*acts, *weights, *extras)


def _split3(x):
    hi = x.astype(BF)
    r = x - hi.astype(F32)
    mid = r.astype(BF)
    lo = (r - mid.astype(F32)).astype(BF)
    return hi, mid, lo


def _gla_masks(r, c):
    row = lax.broadcasted_iota(jnp.int32, (r, r), 0)
    col = lax.broadcasted_iota(jnp.int32, (r, r), 1)
    rid = lax.broadcasted_iota(jnp.int32, (r, 1), 0)
    levels = []
    size = r
    while size > c:
        half = size // 2
        mask = jnp.logical_and(row // size == col // size,
                               jnp.logical_and(row % size >= half, col % size < half))
        levels.append((size, rid % size >= half, mask))
        size = half
    return row >= col, levels, jnp.logical_and(row >= col, row // c == col // c)


def _gla_rows(st, qa, fr, ia, lb, masks, c, valid=None):
    r = qa.shape[0]
    tri, levels, chunk_mask = masks
    e = jnp.exp(-jnp.abs(fr))
    inv = 1.0 / (1.0 + e)
    pos = fr >= 0
    sig = jnp.where(pos, inv, e * inv)
    nsig = jnp.where(pos, e * inv, inv)
    logf = jnp.log(lb + (1.0 - lb) * sig)
    k = (1.0 - lb) * nsig
    if valid is not None:
        logf = jnp.where(valid, logf, 0.0)
        k = jnp.where(valid, k, 0.0)
    q = qa * _sigmoid(qa)
    v = ia.astype(BF)
    tri_b = tri.astype(BF)
    hi, mid, lo = _split3(logf)
    b = (jnp.dot(tri_b, hi, preferred_element_type=F32) + jnp.dot(tri_b, mid, preferred_element_type=F32)
         + jnp.dot(tri_b, lo, preferred_element_type=F32))

    def at_row(size, i):
        b3 = b.reshape(r // size, size, DH)
        return jnp.broadcast_to(b3[:, i:i + 1, :], b3.shape).reshape(r, DH)

    bm = at_row(c, c // 2 - 1)
    qh = (q * jnp.exp(jnp.minimum(b - bm, EXP_CLAMP))).astype(BF)
    kh = (k * jnp.exp(jnp.minimum(bm - b, EXP_CLAMP))).astype(BF)
    attn = jnp.where(chunk_mask, lax.dot_general(qh, kh, _NT, preferred_element_type=F32), 0.0)
    for size, upper, mask in levels:
        bs = at_row(size, size // 2 - 1)
        x = (jnp.where(upper, q, k) * jnp.exp(jnp.where(upper, b - bs, bs - b))).astype(BF)
        attn = attn + jnp.where(mask, lax.dot_general(x, x, _NT, preferred_element_type=F32), 0.0)
    qs = (q * jnp.exp(b)).astype(BF)
    o = (jnp.dot(attn.astype(BF), v, preferred_element_type=F32)
         + lax.dot_general(qs, st.astype(BF), _NT, preferred_element_type=F32))
    bl = b[r - 1:r, :]
    kt = (k * jnp.exp(bl - b)).astype(BF)
    st = st * jnp.exp(bl) + lax.dot_general(v, kt, _TN, preferred_element_type=F32)
    return st, o


def _gla_out(o, og, gn):
    return (_row_rmsnorm(o, gn) * (og * _sigmoid(og))).astype(BF)


def _gla_prompt_kernel(qa_ref, fa_ref, ia_ref, og_ref, mfa_ref, mia_ref, lb_ref, gn_ref,
                       oa_ref, s_ref, *, n_iters):
    lb = lb_ref[...]
    gn = gn_ref[...]
    st, _ = _gla_rows(jnp.zeros((DH, DH), F32), jnp.zeros((N_META, DH), F32), mfa_ref[...], mia_ref[...], lb,
                      _gla_masks(N_META, GLA_CHUNK), GLA_CHUNK)
    masks = _gla_masks(GLA_ROWS, GLA_CHUNK)

    def body(i, st):
        rows = pl.ds(pl.multiple_of(i * GLA_ROWS, GLA_ROWS), GLA_ROWS)
        st, o = _gla_rows(st, qa_ref[rows, :], fa_ref[rows, :], ia_ref[rows, :], lb, masks, GLA_CHUNK)
        oa_ref[rows, :] = _gla_out(o, og_ref[rows, :], gn)
        return st

    st = lax.fori_loop(0, n_iters, body, st, unroll=2)
    s_ref[...] = st.T


def _gla_prompt(proj, proj_meta, lb, g_norm, nb, t):
    blk = lambda cb: pl.BlockSpec((t, DH), lambda n, h, cb=cb: (n, cb * HEADS + h))
    mblk = lambda cb: pl.BlockSpec((N_META, DH), lambda n, h, cb=cb: (0, cb * HEADS + h))
    return pl.pallas_call(
        functools.partial(_gla_prompt_kernel, n_iters=t // GLA_ROWS),
        grid=(nb, HEADS),
        in_specs=[blk(COL_QA), blk(COL_FA), blk(COL_IA), blk(COL_OG), mblk(COL_FA), mblk(COL_IA),
                  pl.BlockSpec((1, DH), lambda n, h: (0, h)), pl.BlockSpec((1, DH), lambda n, h: (0, 0))],
        out_specs=[pl.BlockSpec((t, DH), lambda n, h: (n, h)),
                   pl.BlockSpec((None, None, DH, DH), lambda n, h: (n, h, 0, 0))],
        out_shape=[jax.ShapeDtypeStruct((nb * t, HW), BF),
                   jax.ShapeDtypeStruct((nb, HEADS, DH, DH), F32)],
        compiler_params=_params(("parallel", "parallel")),
        name="gla_prompt",
    )(proj, proj, proj, proj, proj_meta, proj_meta, lb.reshape(1, HW), g_norm.reshape(1, DH))


def _gla_sample_kernel(qa_ref, fa_ref, ia_ref, og_ref, s0_ref, lb_ref, gn_ref, oa_ref, s_ref, *, tn):
    c = SUBLANES
    masks = _gla_masks(c, c)
    gn = gn_ref[...]
    valid = lax.broadcasted_iota(jnp.int32, (c, DH), 0) < tn
    pad = jnp.zeros((c - tn, DH), F32)
    for h in range(HEADS):
        cols = slice(h * DH, (h + 1) * DH)
        qa = jnp.concatenate([qa_ref[:, cols], pad], axis=0)
        fr = jnp.concatenate([fa_ref[:, cols], pad], axis=0)
        ia = jnp.concatenate([ia_ref[:, cols], pad], axis=0)
        og = jnp.concatenate([og_ref[:, cols], pad], axis=0)
        st, o = _gla_rows(s0_ref[h].T, qa, fr, ia, lb_ref[:, cols], masks, c, valid)
        oa_ref[:, cols] = _gla_out(o, og, gn)[:tn, :]
        s_ref[h] = st.T


def _gla_sample(proj3, state, lb, g_norm):
    db, tn, _ = proj3.shape
    assert tn <= SUBLANES
    blk = lambda cb: pl.BlockSpec((None, tn, HW), lambda b, cb=cb: (b, 0, cb))
    sblk = pl.BlockSpec((None, HEADS, DH, DH), lambda b: (b, 0, 0, 0))
    return pl.pallas_call(
        functools.partial(_gla_sample_kernel, tn=tn),
        grid=(db,),
        in_specs=[blk(COL_QA), blk(COL_FA), blk(COL_IA), blk(COL_OG), sblk,
                  pl.BlockSpec((1, HW), lambda b: (0, 0)), pl.BlockSpec((1, DH), lambda b: (0, 0))],
        out_specs=[pl.BlockSpec((None, tn, HW), lambda b: (b, 0, 0)), sblk],
        out_shape=[jax.ShapeDtypeStruct((db, tn, HW), BF),
                   jax.ShapeDtypeStruct((db, HEADS, DH, DH), F32)],
        compiler_params=_params(("parallel",)),
        name="gla_sample",
    )(proj3, proj3, proj3, proj3, state, lb.reshape(1, HW), g_norm.reshape(1, DH))


def _sb_weights(z, c, w2, mask):
    kk = z.shape[1]
    log_beta = jnp.minimum(z, 0.0) - jnp.log(1.0 + jnp.exp(-jnp.abs(z)))
    l1m = log_beta - z
    if mask is not None:
        l1m = jnp.where(mask, l1m, 0.0)
    hi = l1m.astype(BF)
    lo = (l1m - hi.astype(F32)).astype(BF)
    r = jnp.dot(jnp.concatenate([hi, lo], axis=1), w2, preferred_element_type=F32)
    a = jnp.exp(log_beta + r[:, :kk] + c)
    if mask is not None:
        a = jnp.where(mask, a, 0.0)
    return a, c + r[:, kk:]


def _sb_w2(kk):
    row = lax.broadcasted_iota(jnp.int32, (2 * kk, 2 * kk), 0) % kk
    col = lax.broadcasted_iota(jnp.int32, (2 * kk, 2 * kk), 1)
    return jnp.logical_or(col >= kk, row > col).astype(BF)


def _sb_prompt_kernel(qb_ref, kb_ref, vb_ref, mk_ref, mv_ref, gq_ref, gk_ref, bias_ref,
                      ob_ref, ko_ref, vo_ref, q_s, k_s, v_s, mk_s, mv_s, w_s, c_s, acc_s):
    it = pl.program_id(2)
    blk = SB_BLOCK
    nsub = SB_QTILE // blk
    scale = 1.0 / math.sqrt(DH)

    @pl.when(it == 0)
    def _():
        kn = _row_rmsnorm(kb_ref[...], gk_ref[...])
        ko_ref[N_META:, :] = kn
        k_s[...] = kn.astype(BF)
        mkn = _row_rmsnorm(mk_ref[...], gk_ref[...])
        ko_ref[:N_META, :] = mkn
        mk_s[...] = jnp.zeros(mk_s.shape, BF)
        mk_s[:N_META, :] = mkn.astype(BF)
        v = vb_ref[...]
        vo_ref[N_META:, :] = v
        v_s[...] = v.astype(BF)
        mv = mv_ref[...]
        vo_ref[:N_META, :] = mv
        mv_s[...] = jnp.zeros(mv_s.shape, BF)
        mv_s[:N_META, :] = mv.astype(BF)
        q_s[...] = _row_rmsnorm(qb_ref[...], gq_ref[...]).astype(BF)
        w_s[...] = _sb_w2(blk)

    bias = bias_ref[...]
    w2 = w_s[...]
    c_s[...] = jnp.zeros(c_s.shape, F32)
    acc_s[...] = jnp.zeros(acc_s.shape, F32)

    def sweep(r0, kj, vj, mask):
        rows = pl.ds(r0, SB_QTILE - r0)
        qi = q_s[pl.ds(pl.multiple_of(it * SB_QTILE, SB_QTILE) + r0, SB_QTILE - r0), :]
        z = lax.dot_general(qi, kj, _NT, preferred_element_type=F32) * scale + bias
        a, c = _sb_weights(z, c_s[rows, :], w2, mask)
        c_s[rows, :] = c
        acc_s[rows, :] += jnp.dot(a.astype(BF), vj, preferred_element_type=F32)

    def key_rows(j):
        return pl.ds(pl.multiple_of(j * blk, blk), blk)

    for s in reversed(range(nsub)):
        nrow = SB_QTILE - s * blk
        mask = (lax.broadcasted_iota(jnp.int32, (nrow, blk), 1) < lax.broadcasted_iota(jnp.int32, (nrow, blk), 0))
        j = it * nsub + s
        sweep(s * blk, k_s[key_rows(j), :], v_s[key_rows(j), :], mask)

    def body(jj, carry):
        for u in range(nsub):
            j = (it - jj) * nsub - 1 - u
            sweep(0, k_s[key_rows(j), :], v_s[key_rows(j), :], None)
        return carry

    lax.fori_loop(0, it, body, 0)
    sweep(0, mk_s[...], mv_s[...], lax.broadcasted_iota(jnp.int32, (SB_QTILE, blk), 1) < N_META)
    ob_ref[...] = acc_s[...].astype(ob_ref.dtype)


def _sb_prompt(proj, proj_meta, gq, gk, bias, nb, t):
    assert t % SB_QTILE == 0
    nq = t // SB_QTILE
    blk = lambda cb: pl.BlockSpec((t, DH), lambda n, h, i, cb=cb: (n, cb * HEADS + h))
    mblk = lambda cb: pl.BlockSpec((N_META, DH), lambda n, h, i, cb=cb: (0, cb * HEADS + h))
    vec = pl.BlockSpec((1, DH), lambda n, h, i: (0, 0))
    kv_out = pl.BlockSpec((None, N_META + t, DH), lambda n, h, i: (n, 0, h))
    bias_b = jnp.broadcast_to(bias.astype(F32).reshape(HEADS, 1, 1), (HEADS, 1, SB_BLOCK))
    return pl.pallas_call(
        _sb_prompt_kernel,
        grid=(nb, HEADS, nq),
        in_specs=[blk(COL_QB), blk(COL_KB), blk(COL_VB), mblk(COL_KB), mblk(COL_VB), vec, vec,
                  pl.BlockSpec((None, 1, SB_BLOCK), lambda n, h, i: (h, 0, 0))],
        out_specs=[pl.BlockSpec((SB_QTILE, DH), lambda n, h, i: (n * nq + i, h)), kv_out, kv_out],
        out_shape=[jax.ShapeDtypeStruct((nb * t, HW), BF),
                   jax.ShapeDtypeStruct((nb, N_META + t, HW), F32),
                   jax.ShapeDtypeStruct((nb, N_META + t, HW), F32)],
        scratch_shapes=[pltpu.VMEM((t, DH), BF), pltpu.VMEM((t, DH), BF), pltpu.VMEM((t, DH), BF),
                        pltpu.VMEM((SB_BLOCK, DH), BF), pltpu.VMEM((SB_BLOCK, DH), BF),
                        pltpu.VMEM((2 * SB_BLOCK, 2 * SB_BLOCK), BF),
                        pltpu.VMEM((SB_QTILE, SB_BLOCK), F32), pltpu.VMEM((SB_QTILE, DH), F32)],
        compiler_params=_params(("parallel", "parallel", "arbitrary")),
        name="sb_prompt",
    )(proj, proj, proj, proj_meta, proj_meta, gq.reshape(1, DH), gk.reshape(1, DH), bias_b)


def _sb_sample_kernel(pt_ref, qb_ref, kb_ref, vb_ref, *refs, tn, n_steps, pages_per_step):
    page_refs = refs[:4 * pages_per_step]
    gq_ref, gk_ref, bias_ref, ob_ref, ko_ref, vo_ref, q_s, c_s, acc_s, w_s = refs[4 * pages_per_step:]
    p = pl.program_id(1)
    rq = 16
    ro = SUBLANES
    half = HEADS // 2
    pg = page_refs[0].shape[0]
    scale = 1.0 / math.sqrt(DH)
    zrows = jnp.zeros((rq - ro, pg), F32)

    def sweep(get_k, get_v, mask):
        z = jnp.concatenate(
            [lax.dot_general(q_s[h * rq:(h + 1) * rq, :], get_k(h), _NT, preferred_element_type=F32)[:ro]
             for h in range(HEADS)], axis=0) * scale + bias_ref[...]
        a, c = _sb_weights(z, c_s[...], w_s[...], mask)
        c_s[...] = c
        acc_s[...] += jnp.concatenate(
            [jnp.dot(jnp.concatenate([a[h * ro:(h + 1) * ro], zrows], axis=0).astype(BF), get_v(h),
                     preferred_element_type=F32)[:ro] for h in range(HEADS)], axis=0)

    @pl.when(p == 0)
    def _():
        w_s[...] = _sb_w2(pg)
        c_s[...] = jnp.zeros(c_s.shape, F32)
        acc_s[...] = jnp.zeros(acc_s.shape, F32)
        q_s[...] = jnp.zeros(q_s.shape, BF)
        vo_ref[...] = vb_ref[...]
        kpad = jnp.zeros((pg - tn, DH), BF)
        ks, vs = [], []
        for h in range(HEADS):
            cols = slice(h * DH, (h + 1) * DH)
            q_s[h * rq:h * rq + tn, :] = _row_rmsnorm(qb_ref[:, cols], gq_ref[...]).astype(BF)
            kn = _row_rmsnorm(kb_ref[:, cols], gk_ref[...])
            ko_ref[:, cols] = kn
            ks.append(jnp.concatenate([kn.astype(BF), kpad], axis=0))
            vs.append(jnp.concatenate([vb_ref[:, cols].astype(BF), kpad], axis=0))
        row = lax.broadcasted_iota(jnp.int32, (HEADS * ro, pg), 0) % ro
        col = lax.broadcasted_iota(jnp.int32, (HEADS * ro, pg), 1)
        sweep(lambda h: ks[h], lambda h: vs[h], jnp.logical_and(col < row, col < tn))

    for s in range(pages_per_step):
        k_lo, k_hi, v_lo, v_hi = [r.reshape(pg * half, DH) for r in page_refs[4 * s:4 * s + 4]]
        head = lambda lo, hi, h: (lo if h < half else hi)[pl.ds(h % half, pg, stride=half), :].astype(BF)
        sweep(functools.partial(head, k_lo, k_hi), functools.partial(head, v_lo, v_hi), None)

    @pl.when(p == n_steps - 1)
    def _():
        for h in range(HEADS):
            ob_ref[:, h * DH:(h + 1) * DH] = acc_s[h * ro:h * ro + tn, :].astype(ob_ref.dtype)


def _sb_sample(proj3, cache_k, cache_v, page_table, gq, gk, bias):
    db, tn, _ = proj3.shape
    n_pages = page_table.shape[1]
    pg = cache_k.shape[1]
    pps = SB_PAGES_PER_STEP if n_pages % SB_PAGES_PER_STEP == 0 else 1
    n_steps = n_pages // pps
    half = HEADS // 2
    assert tn <= SUBLANES and half == SUBLANES
    blk = lambda cb: pl.BlockSpec((None, tn, HW), lambda b, p, pt, cb=cb: (b, 0, cb))

    def page(s, g):
        return pl.BlockSpec((None, pg, half, DH),
                            lambda b, p, pt, s=s, g=g: (pt[b, n_pages - 1 - (p * pps + s)], 0, g, 0))

    pages, page_specs = [], []
    for s in range(pps):
        pages += [cache_k, cache_k, cache_v, cache_v]
        page_specs += [page(s, 0), page(s, 1), page(s, 0), page(s, 1)]
    vec = pl.BlockSpec((1, DH), lambda b, p, pt: (0, 0))
    out = pl.BlockSpec((None, tn, HW), lambda b, p, pt: (b, 0, 0))
    bias_b = jnp.broadcast_to(bias.astype(F32).reshape(HEADS, 1, 1), (HEADS, SUBLANES, pg)).reshape(HEADS * SUBLANES, pg)
    return pl.pallas_call(
        functools.partial(_sb_sample_kernel, tn=tn, n_steps=n_steps, pages_per_step=pps),
        grid_spec=pltpu.PrefetchScalarGridSpec(
            num_scalar_prefetch=1,
            grid=(db, n_steps),
            in_specs=[blk(COL_QB), blk(COL_KB), blk(COL_VB)] + page_specs + [
                vec, vec, pl.BlockSpec((HEADS * SUBLANES, pg), lambda b, p, pt: (0, 0))],
            out_specs=[out, out, out],
            scratch_shapes=[pltpu.VMEM((HEADS * 16, DH), BF), pltpu.VMEM((HEADS * SUBLANES, pg), F32),
                            pltpu.VMEM((HEADS * SUBLANES, DH), F32), pltpu.VMEM((2 * pg, 2 * pg), BF)]),
        out_shape=[jax.ShapeDtypeStruct((db, tn, HW), BF),
                   jax.ShapeDtypeStruct((db, tn, HW), F32),
                   jax.ShapeDtypeStruct((db, tn, HW), F32)],
        compiler_params=_params(("parallel", "arbitrary")),
        name="sb_sample",
    )(page_table, proj3, proj3, proj3, *pages, gq.reshape(1, DH), gk.reshape(1, DH), bias_b)


def _finish(x, oa, ob, proj, w_pa, w_pb, w_o, g_ffn, w_gate, w_up, w_down, tm):
    d = x.shape[1]
    merged = _dense("gated_sum", [oa, ob], [w_pa, w_pb], [proj, proj], [COL_GTA, COL_GTA + d], BF,
                    tm, 512, "merge_proj")
    x1 = _dense("residual", [merged], [w_o], [x], [0], F32, tm, 512, "out_proj")
    h = _rmsnorm_bf16(x1, g_ffn)
    act = _dense("swiglu", [h], [w_gate, w_up], [], [], BF, tm, 512, "ffn_up")
    return _dense("residual", [act], [w_down], [x1], [0], F32, min(tm, 512), 512, "ffn_down")


def kernel(x_prompt, x_sample, cache_k, cache_v, state_hgrn, page_table, meta_tokens, g_mix, w_in,
           hgrn_lb_logits, hgrn_g_norm, sb_gq, sb_gk, sb_bias, w_pa, w_pb, w_o, g_ffn, w_gate, w_up, w_down):
    nb, t, d = x_prompt.shape
    db, tn, _ = x_sample.shape
    depth = w_in.shape[0]
    assert depth == 1, "single-layer trunk"
    assert t % GLA_ROWS == 0 and N_META % GLA_CHUNK == 0
    l = 0
    lb = jnp.cumsum(jax.nn.softmax(hgrn_lb_logits.astype(F32), axis=0), axis=0)[l]
    lead = lambda a: a.reshape(a.shape[1:])
    wb = lambda w: lead(w).astype(BF)
    w_in_b = wb(w_in)
    lw = (wb(w_pa), wb(w_pb), wb(w_o), g_ffn[l], wb(w_gate), wb(w_up), wb(w_down))

    xp = x_prompt.reshape(nb * t, d)
    xs = x_sample.reshape(db * tn, d)

    def in_proj(x, tm):
        return _dense("plain", [_rmsnorm_bf16(x, g_mix[l])], [w_in_b], [], [], F32, tm, 1024, "in_proj")

    proj_m = in_proj(meta_tokens.astype(x_prompt.dtype), N_META)
    proj_p = in_proj(xp, 1024)
    proj_s = in_proj(xs, 512)
    proj_s3 = proj_s.reshape(db, tn, -1)

    oa_p, s_p = _gla_prompt(proj_p, proj_m, lb, hgrn_g_norm[l], nb, t)
    ob_p, k_p, v_p = _sb_prompt(proj_p, proj_m, sb_gq[l], sb_gk[l], sb_bias[l], nb, t)
    y_p = _finish(xp, oa_p, ob_p, proj_p, *lw, tm=1024)

    oa_s, s_s = _gla_sample(proj_s3, lead(state_hgrn), lb, hgrn_g_norm[l])
    ob_s, k_s, v_s = _sb_sample(proj_s3, lead(cache_k), lead(cache_v), page_table, sb_gq[l], sb_gk[l],
                                sb_bias[l])
    y_s = _finish(xs, oa_s.reshape(db * tn, HW), ob_s.reshape(db * tn, HW), proj_s, *lw, tm=512)

    return (y_p.reshape(nb, t, d), y_s.reshape(db, tn, d),
            k_p.reshape(1, nb, N_META + t, HEADS, DH), v_p.reshape(1, nb, N_META + t, HEADS, DH),
            s_p[None],
            k_s.reshape(1, db, tn, HEADS, DH).astype(cache_k.dtype),
            v_s.reshape(1, db, tn, HEADS, DH).astype(cache_v.dtype),
            s_s[None].astype(state_hgrn.dtype))
```

```python
import functools
import math

import jax
import jax.numpy as jnp
from jax import lax
from jax.experimental import pallas as pl
from jax.experimental.pallas import tpu as pltpu

N_META = 16
HEADS = 16
DH = 128
HW = HEADS * DH
EPS = 1e-6
SUBLANES = 8
GLA_CHUNK = 16
GLA_ROWS = 128
GLA_UNROLL = 8
SB_BLOCK = 128
SB_QTILE = 2048
SB_PAGES_PER_STEP = 4
EXP_CLAMP = 80.0
VMEM_LIMIT = 56 * 1024 * 1024

COL_QA, COL_FA, COL_IA, COL_OG, COL_QB, COL_KB, COL_VB = range(7)
COL_GTA = 7 * HW

_NT = (((1,), (1,)), ((), ()))
_TN = (((0,), (0,)), ((), ()))
BF = jnp.bfloat16
F32 = jnp.float32


def _params(sem):
    return pltpu.CompilerParams(dimension_semantics=sem, vmem_limit_bytes=VMEM_LIMIT)


def _sigmoid(x):
    return 1.0 / (1.0 + jnp.exp(-x))


def _row_rmsnorm(x, g):
    return x * lax.rsqrt(jnp.mean(x * x, axis=-1, keepdims=True) + EPS) * g


def _rmsnorm_kernel(x_ref, g_ref, o_ref):
    o_ref[...] = _row_rmsnorm(x_ref[...], g_ref[...]).astype(o_ref.dtype)


def _rmsnorm_bf16(x, g, tm=256):
    m, d = x.shape
    tm = min(tm, m)
    return pl.pallas_call(
        _rmsnorm_kernel,
        grid=(pl.cdiv(m, tm),),
        in_specs=[pl.BlockSpec((tm, d), lambda i: (i, 0)), pl.BlockSpec((1, d), lambda i: (0, 0))],
        out_specs=pl.BlockSpec((tm, d), lambda i: (i, 0)),
        out_shape=jax.ShapeDtypeStruct((m, d), BF),
        compiler_params=_params(("parallel",)),
        name="rmsnorm_bf16",
    )(x, g.reshape(1, d))


def _dense_kernel(*refs, mode):
    o_ref = refs[-1]
    if mode == "plain":
        a, w = refs[:2]
        o_ref[...] = jnp.dot(a[...], w[...], preferred_element_type=F32).astype(o_ref.dtype)
    elif mode == "residual":
        a, w, r = refs[:3]
        o_ref[...] = r[...] + jnp.dot(a[...], w[...], preferred_element_type=F32)
    elif mode == "swiglu":
        a, wg, wu = refs[:3]
        g = jnp.dot(a[...], wg[...], preferred_element_type=F32)
        u = jnp.dot(a[...], wu[...], preferred_element_type=F32)
        o_ref[...] = (g * _sigmoid(g) * u).astype(o_ref.dtype)
    elif mode == "gated_sum":
        a, b, wa, wb, ga, gb = refs[:6]
        pa = jnp.dot(a[...], wa[...], preferred_element_type=F32)
        pb = jnp.dot(b[...], wb[...], preferred_element_type=F32)
        o_ref[...] = (_sigmoid(ga[...]) * pa + _sigmoid(gb[...]) * pb).astype(o_ref.dtype)
    else:
        raise ValueError(mode)


def _dense(mode, acts, weights, extras, extra_col_offsets, out_dtype, tm, tn, name):
    m = acts[0].shape[0]
    n = weights[0].shape[1]
    tm = min(tm, m)
    tn = min(tn, n)
    in_specs = [pl.BlockSpec((tm, a.shape[1]), lambda i, j: (i, 0)) for a in acts]
    in_specs += [pl.BlockSpec((w.shape[0], tn), lambda i, j: (0, j)) for w in weights]
    for off in extra_col_offsets:
        assert off % tn == 0
        in_specs.append(pl.BlockSpec((tm, tn), lambda i, j, o=off // tn: (i, j + o)))
    return pl.pallas_call(
        functools.partial(_dense_kernel, mode=mode),
        grid=(pl.cdiv(m, tm), pl.cdiv(n, tn)),
        in_specs=in_specs,
        out_specs=pl.BlockSpec((tm, tn), lambda i, j: (i, j)),
        out_shape=jax.ShapeDtypeStruct((m, n), out_dtype),
        compiler_params=_params(("parallel", "arbitrary")),
        name=name,
    )(*acts, *weights, *extras)


def _split3(x):
    hi = x.astype(BF)
    r = x - hi.astype(F32)
    mid = r.astype(BF)
    lo = (r - mid.astype(F32)).astype(BF)
    return hi, mid, lo


def _gla_masks(r, c):
    row = lax.broadcasted_iota(jnp.int32, (r, r), 0)
    col = lax.broadcasted_iota(jnp.int32, (r, r), 1)
    rid = lax.broadcasted_iota(jnp.int32, (r, 1), 0)
    levels = []
    size = r
    while size > c:
        half = size // 2
        mask = jnp.logical_and(row // size == col // size,
                               jnp.logical_and(row % size >= half, col % size < half))
        levels.append((size, rid % size >= half, mask))
        size = half
    return row >= col, levels, jnp.logical_and(row >= col, row // c == col // c)


def _gla_front(qa, fr, ia, lb, masks, c, valid=None):
    r = qa.shape[0]
    tri, levels, chunk_mask = masks
    e = jnp.exp(-jnp.abs(fr))
    inv = 1.0 / (1.0 + e)
    pos = fr >= 0
    sig = jnp.where(pos, inv, e * inv)
    nsig = jnp.where(pos, e * inv, inv)
    logf = jnp.log(lb + (1.0 - lb) * sig)
    k = (1.0 - lb) * nsig
    if valid is not None:
        logf = jnp.where(valid, logf, 0.0)
        k = jnp.where(valid, k, 0.0)
    q = qa * _sigmoid(qa)
    v = ia.astype(BF)
    tri_b = tri.astype(BF)
    hi, mid, lo = _split3(logf)
    b = (jnp.dot(tri_b, hi, preferred_element_type=F32) + jnp.dot(tri_b, mid, preferred_element_type=F32)
         + jnp.dot(tri_b, lo, preferred_element_type=F32))

    def at_row(size, i):
        b3 = b.reshape(r // size, size, DH)
        return jnp.broadcast_to(b3[:, i:i + 1, :], b3.shape).reshape(r, DH)

    bm = at_row(c, c // 2 - 1)
    qh = (q * jnp.exp(jnp.minimum(b - bm, EXP_CLAMP))).astype(BF)
    kh = (k * jnp.exp(jnp.minimum(bm - b, EXP_CLAMP))).astype(BF)
    attn = jnp.where(chunk_mask, lax.dot_general(qh, kh, _NT, preferred_element_type=F32), 0.0)
    for size, upper, mask in levels:
        bs = at_row(size, size // 2 - 1)
        x = (jnp.where(upper, q, k) * jnp.exp(jnp.where(upper, b - bs, bs - b))).astype(BF)
        attn = attn + jnp.where(mask, lax.dot_general(x, x, _NT, preferred_element_type=F32), 0.0)
    return q, k, v, b, jnp.dot(attn.astype(BF), v, preferred_element_type=F32), at_row


def _gla_rows(st, qa, fr, ia, lb, masks, c):
    r = qa.shape[0]
    q, k, v, b, o, _ = _gla_front(qa, fr, ia, lb, masks, c)
    qs = (q * jnp.exp(b)).astype(BF)
    bl = b[r - 1:r, :]
    kt = (k * jnp.exp(bl - b)).astype(BF)
    o = o + lax.dot_general(qs, st.astype(BF), _NT, preferred_element_type=F32)
    st = st * jnp.exp(bl) + lax.dot_general(v, kt, _TN, preferred_element_type=F32)
    return st, o


def _gla_out(o, og, gn):
    return (_row_rmsnorm(o, gn) * (og * _sigmoid(og))).astype(BF)


def _gla_prompt_kernel(qa_ref, fa_ref, ia_ref, og_ref, mfa_ref, mia_ref, lb_ref, gn_ref,
                       oa_ref, s_ref, *, n_iters):
    lb = lb_ref[...]
    gn = gn_ref[...]
    st, _ = _gla_rows(jnp.zeros((DH, DH), F32), jnp.zeros((N_META, DH), F32), mfa_ref[...], mia_ref[...], lb,
                      _gla_masks(N_META, GLA_CHUNK), GLA_CHUNK)
    masks = _gla_masks(GLA_ROWS, GLA_CHUNK)

    def body(i, st):
        rows = pl.ds(pl.multiple_of(i * GLA_ROWS, GLA_ROWS), GLA_ROWS)
        st, o = _gla_rows(st, qa_ref[rows, :], fa_ref[rows, :], ia_ref[rows, :], lb, masks, GLA_CHUNK)
        oa_ref[rows, :] = _gla_out(o, og_ref[rows, :], gn)
        return st

    st = lax.fori_loop(0, n_iters, body, st, unroll=GLA_UNROLL if n_iters % GLA_UNROLL == 0 else 1)
    s_ref[...] = st.T


def _gla_prompt(proj, proj_meta, lb, g_norm, nb, t):
    blk = lambda cb: pl.BlockSpec((t, DH), lambda n, h, cb=cb: (n, cb * HEADS + h))
    mblk = lambda cb: pl.BlockSpec((N_META, DH), lambda n, h, cb=cb: (0, cb * HEADS + h))
    return pl.pallas_call(
        functools.partial(_gla_prompt_kernel, n_iters=t // GLA_ROWS),
        grid=(nb, HEADS),
        in_specs=[blk(COL_QA), blk(COL_FA), blk(COL_IA), blk(COL_OG), mblk(COL_FA), mblk(COL_IA),
                  pl.BlockSpec((1, DH), lambda n, h: (0, h)), pl.BlockSpec((1, DH), lambda n, h: (0, 0))],
        out_specs=[pl.BlockSpec((t, DH), lambda n, h: (n, h)),
                   pl.BlockSpec((None, None, DH, DH), lambda n, h: (n, h, 0, 0))],
        out_shape=[jax.ShapeDtypeStruct((nb * t, HW), BF),
                   jax.ShapeDtypeStruct((nb, HEADS, DH, DH), F32)],
        compiler_params=_params(("parallel", "parallel")),
        name="gla_prompt",
    )(proj, proj, proj, proj, proj_meta, proj_meta, lb.reshape(1, HW), g_norm.reshape(1, DH))


def _gla_sample_kernel(qa_ref, fa_ref, ia_ref, og_ref, s0_ref, lb_ref, gn_ref, oa_ref, s_ref, *, tn):
    c = SUBLANES
    r = HEADS * c
    row = lax.broadcasted_iota(jnp.int32, (r, r), 0)
    col = lax.broadcasted_iota(jnp.int32, (r, r), 1)
    own = jnp.logical_and(row >= col, row // c == col // c)
    valid = lax.broadcasted_iota(jnp.int32, (r, DH), 0) % c < tn
    pad = jnp.zeros((c - tn, DH), F32)

    def stack(ref):
        return jnp.concatenate([x for h in range(HEADS) for x in (ref[:, h * DH:(h + 1) * DH], pad)], axis=0)

    lb = jnp.concatenate([jnp.broadcast_to(lb_ref[:, h * DH:(h + 1) * DH], (c, DH)) for h in range(HEADS)], axis=0)
    q, k, v, b, o, at_row = _gla_front(stack(qa_ref), stack(fa_ref), stack(ia_ref), lb, (own, [], own), c, valid)
    qs = (q * jnp.exp(b)).astype(BF)
    bl = at_row(c, c - 1)
    kt = (k * jnp.exp(bl - b)).astype(BF)
    decay = jnp.exp(bl)
    o_state = []
    for h in range(HEADS):
        rows = slice(h * c, (h + 1) * c)
        st = s0_ref[h]
        o_state.append(jnp.dot(qs[rows], st.astype(BF), preferred_element_type=F32))
        decay_col = decay[rows].T[:, 0:1]
        s_ref[h] = st * decay_col + lax.dot_general(kt[rows], v[rows], _TN, preferred_element_type=F32)
    out = _gla_out(o + jnp.concatenate(o_state, axis=0), stack(og_ref), gn_ref[...])
    for h in range(HEADS):
        oa_ref[:, h * DH:(h + 1) * DH] = out[h * c:h * c + tn, :]


def _gla_sample(proj3, state, lb, g_norm):
    db, tn, _ = proj3.shape
    assert tn <= SUBLANES
    blk = lambda cb: pl.BlockSpec((None, tn, HW), lambda b, cb=cb: (b, 0, cb))
    sblk = pl.BlockSpec((None, HEADS, DH, DH), lambda b: (b, 0, 0, 0))
    return pl.pallas_call(
        functools.partial(_gla_sample_kernel, tn=tn),
        grid=(db,),
        in_specs=[blk(COL_QA), blk(COL_FA), blk(COL_IA), blk(COL_OG), sblk,
                  pl.BlockSpec((1, HW), lambda b: (0, 0)), pl.BlockSpec((1, DH), lambda b: (0, 0))],
        out_specs=[pl.BlockSpec((None, tn, HW), lambda b: (b, 0, 0)), sblk],
        out_shape=[jax.ShapeDtypeStruct((db, tn, HW), BF),
                   jax.ShapeDtypeStruct((db, HEADS, DH, DH), F32)],
        compiler_params=_params(("parallel",)),
        name="gla_sample",
    )(proj3, proj3, proj3, proj3, state, lb.reshape(1, HW), g_norm.reshape(1, DH))


def _sb_tail(z, w2, mask):
    kk = z.shape[1]
    log_beta = jnp.minimum(z, 0.0) - jnp.log(1.0 + jnp.exp(-jnp.abs(z)))
    l1m = log_beta - z
    if mask is not None:
        l1m = jnp.where(mask, l1m, 0.0)
    hi = l1m.astype(BF)
    lo = (l1m - hi.astype(F32)).astype(BF)
    r = jnp.dot(jnp.concatenate([hi, lo], axis=1), w2, preferred_element_type=F32)
    return log_beta + r[:, :kk], r[:, kk:]


def _sb_exp(logw, c, mask):
    a = jnp.exp(logw + c)
    return a if mask is None else jnp.where(mask, a, 0.0)


def _sb_w2(kk):
    row = lax.broadcasted_iota(jnp.int32, (2 * kk, 2 * kk), 0) % kk
    col = lax.broadcasted_iota(jnp.int32, (2 * kk, 2 * kk), 1)
    return jnp.logical_or(col >= kk, row > col).astype(BF)


def _sb_prompt_kernel(qb_ref, kb_ref, vb_ref, mk_ref, mv_ref, gq_ref, gk_ref, bias_ref,
                      ob_ref, ko_ref, vo_ref, q_s, k_s, v_s, mk_s, mv_s, w_s, c_s, acc_s, *, qtile):
    it = pl.program_id(2)
    blk = SB_BLOCK
    nsub = qtile // blk
    scale = 1.0 / math.sqrt(DH)

    @pl.when(it == 0)
    def _():
        kn = _row_rmsnorm(kb_ref[...], gk_ref[...])
        ko_ref[N_META:, :] = kn
        k_s[...] = kn.astype(BF)
        mkn = _row_rmsnorm(mk_ref[...], gk_ref[...])
        ko_ref[:N_META, :] = mkn
        mk_s[...] = jnp.zeros(mk_s.shape, BF)
        mk_s[:N_META, :] = mkn.astype(BF)
        v = vb_ref[...]
        vo_ref[N_META:, :] = v
        v_s[...] = v.astype(BF)
        mv = mv_ref[...]
        vo_ref[:N_META, :] = mv
        mv_s[...] = jnp.zeros(mv_s.shape, BF)
        mv_s[:N_META, :] = mv.astype(BF)
        q_s[...] = _row_rmsnorm(qb_ref[...], gq_ref[...]).astype(BF)
        w_s[...] = _sb_w2(blk)

    bias = bias_ref[...]
    w2 = w_s[...]
    c_s[...] = jnp.zeros(c_s.shape, F32)
    acc_s[...] = jnp.zeros(acc_s.shape, F32)

    def sweep(r0, kj, vj, visible):
        nr = qtile - r0
        rows = pl.ds(r0, nr)
        qi = q_s[pl.ds(pl.multiple_of(it * qtile, qtile) + r0, nr), :]
        z = lax.dot_general(qi, kj, _NT, preferred_element_type=F32) * scale + bias
        col = lax.broadcasted_iota(jnp.int32, (nr, blk), 1)
        if visible is None:
            m = None
        elif visible == "causal":
            m = col < lax.broadcasted_iota(jnp.int32, (nr, blk), 0)
        else:
            m = col < visible
        logw, tot = _sb_tail(z, w2, m)
        c = c_s[rows, :]
        a = _sb_exp(logw, c, m)
        c_s[rows, :] = c + tot
        acc_s[rows, :] += jnp.dot(a.astype(BF), vj, preferred_element_type=F32)

    def key_rows(j):
        return pl.ds(pl.multiple_of(j * blk, blk), blk)

    for s in reversed(range(nsub)):
        j = it * nsub + s
        sweep(s * blk, k_s[key_rows(j), :], v_s[key_rows(j), :], "causal")

    def body(jj, carry):
        for u in range(nsub):
            j = (it - jj) * nsub - 1 - u
            sweep(0, k_s[key_rows(j), :], v_s[key_rows(j), :], None)
        return carry

    lax.fori_loop(0, it, body, 0)
    sweep(0, mk_s[...], mv_s[...], N_META)
    ob_ref[...] = acc_s[...].astype(ob_ref.dtype)


def _sb_prompt(proj, proj_meta, gq, gk, bias, nb, t):
    qtile = min(SB_QTILE, t)
    assert t % qtile == 0 and qtile % SB_BLOCK == 0
    nq = t // qtile
    blk = lambda cb: pl.BlockSpec((t, DH), lambda n, h, i, cb=cb: (n, cb * HEADS + h))
    mblk = lambda cb: pl.BlockSpec((N_META, DH), lambda n, h, i, cb=cb: (0, cb * HEADS + h))
    vec = pl.BlockSpec((1, DH), lambda n, h, i: (0, 0))
    kv_out = pl.BlockSpec((None, N_META + t, DH), lambda n, h, i: (n, 0, h))
    bias_b = jnp.broadcast_to(bias.astype(F32).reshape(HEADS, 1, 1), (HEADS, 1, SB_BLOCK))
    return pl.pallas_call(
        functools.partial(_sb_prompt_kernel, qtile=qtile),
        grid=(nb, HEADS, nq),
        in_specs=[blk(COL_QB), blk(COL_KB), blk(COL_VB), mblk(COL_KB), mblk(COL_VB), vec, vec,
                  pl.BlockSpec((None, 1, SB_BLOCK), lambda n, h, i: (h, 0, 0))],
        out_specs=[pl.BlockSpec((qtile, DH), lambda n, h, i: (n * nq + i, h)), kv_out, kv_out],
        out_shape=[jax.ShapeDtypeStruct((nb * t, HW), BF),
                   jax.ShapeDtypeStruct((nb, N_META + t, HW), F32),
                   jax.ShapeDtypeStruct((nb, N_META + t, HW), F32)],
        scratch_shapes=[pltpu.VMEM((t, DH), BF), pltpu.VMEM((t, DH), BF), pltpu.VMEM((t, DH), BF),
                        pltpu.VMEM((SB_BLOCK, DH), BF), pltpu.VMEM((SB_BLOCK, DH), BF),
                        pltpu.VMEM((2 * SB_BLOCK, 2 * SB_BLOCK), BF),
                        pltpu.VMEM((qtile, SB_BLOCK), F32), pltpu.VMEM((qtile, DH), F32)],
        compiler_params=_params(("parallel", "parallel", "arbitrary")),
        name="sb_prompt",
    )(proj, proj, proj, proj_meta, proj_meta, gq.reshape(1, DH), gk.reshape(1, DH), bias_b)


def _sb_sample_kernel(pt_ref, qb_ref, kb_ref, vb_ref, *refs, tn, n_steps, pages_per_step):
    page_refs = refs[:4 * pages_per_step]
    gq_ref, gk_ref, bias_ref, ob_ref, ko_ref, vo_ref, q_s, c_s, acc_s, w_s = refs[4 * pages_per_step:]
    p = pl.program_id(1)
    rq = 16
    ro = SUBLANES
    half = HEADS // 2
    pg = page_refs[0].shape[0]
    scale = 1.0 / math.sqrt(DH)
    zrows = jnp.zeros((rq - ro, pg), F32)

    def sweep(blocks, mask):
        nblk = len(blocks)
        hr = HEADS * ro
        z = jnp.concatenate(
            [lax.dot_general(q_s[h * rq:(h + 1) * rq, :], get_k(h), _NT, preferred_element_type=F32)[:ro]
             for get_k, _ in blocks for h in range(HEADS)], axis=0)
        z = (z.reshape(nblk, hr, pg) * scale + bias_ref[...]).reshape(nblk * hr, pg)
        c0 = c_s[...]
        logw, tot = _sb_tail(z, w_s[...], mask)
        cs = []
        for s in range(nblk):
            cs.append(c0)
            c0 = c0 + tot[s * hr:(s + 1) * hr]
        c_s[...] = c0
        a = _sb_exp(logw, jnp.concatenate(cs, axis=0), mask)
        acc = acc_s[...]
        for s, (_, get_v) in enumerate(blocks):
            acc = acc + jnp.concatenate(
                [jnp.dot(jnp.concatenate([a[s * hr + h * ro:s * hr + (h + 1) * ro], zrows], axis=0).astype(BF),
                         get_v(h), preferred_element_type=F32)[:ro] for h in range(HEADS)], axis=0)
        acc_s[...] = acc

    @pl.when(p == 0)
    def _():
        w_s[...] = _sb_w2(pg)
        c_s[...] = jnp.zeros(c_s.shape, F32)
        acc_s[...] = jnp.zeros(acc_s.shape, F32)
        q_s[...] = jnp.zeros(q_s.shape, BF)
        vo_ref[...] = vb_ref[...]
        kpad = jnp.zeros((pg - tn, DH), BF)
        ks, vs = [], []
        for h in range(HEADS):
            cols = slice(h * DH, (h + 1) * DH)
            q_s[h * rq:h * rq + tn, :] = _row_rmsnorm(qb_ref[:, cols], gq_ref[...]).astype(BF)
            kn = _row_rmsnorm(kb_ref[:, cols], gk_ref[...])
            ko_ref[:, cols] = kn
            ks.append(jnp.concatenate([kn.astype(BF), kpad], axis=0))
            vs.append(jnp.concatenate([vb_ref[:, cols].astype(BF), kpad], axis=0))
        row = lax.broadcasted_iota(jnp.int32, (HEADS * ro, pg), 0) % ro
        col = lax.broadcasted_iota(jnp.int32, (HEADS * ro, pg), 1)
        sweep([(lambda h: ks[h], lambda h: vs[h])], jnp.logical_and(col < row, col < tn))

    def head(lo, hi, h):
        return (lo if h < half else hi)[pl.ds(h % half, pg, stride=half), :].astype(BF)

    blocks = []
    for s in range(pages_per_step):
        k_lo, k_hi, v_lo, v_hi = [r.reshape(pg * half, DH) for r in page_refs[4 * s:4 * s + 4]]
        blocks.append((functools.partial(head, k_lo, k_hi), functools.partial(head, v_lo, v_hi)))
    sweep(blocks, None)

    @pl.when(p == n_steps - 1)
    def _():
        for h in range(HEADS):
            ob_ref[:, h * DH:(h + 1) * DH] = acc_s[h * ro:h * ro + tn, :].astype(ob_ref.dtype)


def _sb_sample(proj3, cache_k, cache_v, page_table, gq, gk, bias):
    db, tn, _ = proj3.shape
    n_pages = page_table.shape[1]
    pg = cache_k.shape[1]
    pps = SB_PAGES_PER_STEP if n_pages % SB_PAGES_PER_STEP == 0 else 1
    n_steps = n_pages // pps
    half = HEADS // 2
    assert tn <= SUBLANES and half == SUBLANES
    blk = lambda cb: pl.BlockSpec((None, tn, HW), lambda b, p, pt, cb=cb: (b, 0, cb))

    def page(s, g):
        return pl.BlockSpec((None, pg, half, DH),
                            lambda b, p, pt, s=s, g=g: (pt[b, n_pages - 1 - (p * pps + s)], 0, g, 0))

    pages, page_specs = [], []
    for s in range(pps):
        pages += [cache_k, cache_k, cache_v, cache_v]
        page_specs += [page(s, 0), page(s, 1), page(s, 0), page(s, 1)]
    vec = pl.BlockSpec((1, DH), lambda b, p, pt: (0, 0))
    out = pl.BlockSpec((None, tn, HW), lambda b, p, pt: (b, 0, 0))
    bias_b = jnp.broadcast_to(bias.astype(F32).reshape(HEADS, 1, 1), (HEADS, SUBLANES, pg)).reshape(HEADS * SUBLANES, pg)
    return pl.pallas_call(
        functools.partial(_sb_sample_kernel, tn=tn, n_steps=n_steps, pages_per_step=pps),
        grid_spec=pltpu.PrefetchScalarGridSpec(
            num_scalar_prefetch=1,
            grid=(db, n_steps),
            in_specs=[blk(COL_QB), blk(COL_KB), blk(COL_VB)] + page_specs + [
                vec, vec, pl.BlockSpec((HEADS * SUBLANES, pg), lambda b, p, pt: (0, 0))],
            out_specs=[out, out, out],
            scratch_shapes=[pltpu.VMEM((HEADS * 16, DH), BF), pltpu.VMEM((HEADS * SUBLANES, pg), F32),
                            pltpu.VMEM((HEADS * SUBLANES, DH), F32), pltpu.VMEM((2 * pg, 2 * pg), BF)]),
        out_shape=[jax.ShapeDtypeStruct((db, tn, HW), BF),
                   jax.ShapeDtypeStruct((db, tn, HW), F32),
                   jax.ShapeDtypeStruct((db, tn, HW), F32)],
        compiler_params=_params(("parallel", "arbitrary")),
        name="sb_sample",
    )(page_table, proj3, proj3, proj3, *pages, gq.reshape(1, DH), gk.reshape(1, DH), bias_b)


def _finish(x, oa, ob, proj, w_pa, w_pb, w_o, g_ffn, w_gate, w_up, w_down, tm):
    d = x.shape[1]
    merged = _dense("gated_sum", [oa, ob], [w_pa, w_pb], [proj, proj], [COL_GTA, COL_GTA + d], BF,
                    tm, 512, "merge_proj")
    x1 = _dense("residual", [merged], [w_o], [x], [0], F32, tm, 512, "out_proj")
    h = _rmsnorm_bf16(x1, g_ffn)
    act = _dense("swiglu", [h], [w_gate, w_up], [], [], BF, tm, 512, "ffn_up")
    return _dense("residual", [act], [w_down], [x1], [0], F32, min(tm, 512), 512, "ffn_down")


def kernel(x_prompt, x_sample, cache_k, cache_v, state_hgrn, page_table, meta_tokens, g_mix, w_in,
           hgrn_lb_logits, hgrn_g_norm, sb_gq, sb_gk, sb_bias, w_pa, w_pb, w_o, g_ffn, w_gate, w_up, w_down):
    nb, t, d = x_prompt.shape
    db, tn, _ = x_sample.shape
    depth = w_in.shape[0]
    assert depth == 1, "single-layer trunk"
    assert t % GLA_ROWS == 0 and N_META % GLA_CHUNK == 0
    l = 0
    lb = jnp.cumsum(jax.nn.softmax(hgrn_lb_logits.astype(F32), axis=0), axis=0)[l]
    lead = lambda a: a.reshape(a.shape[1:])
    wb = lambda w: lead(w).astype(BF)
    w_in_b = wb(w_in)
    lw = (wb(w_pa), wb(w_pb), wb(w_o), g_ffn[l], wb(w_gate), wb(w_up), wb(w_down))

    xp = x_prompt.reshape(nb * t, d)
    xs = x_sample.reshape(db * tn, d)

    def in_proj(x, tm):
        return _dense("plain", [_rmsnorm_bf16(x, g_mix[l])], [w_in_b], [], [], F32, tm, 1024, "in_proj")

    proj_m = in_proj(meta_tokens.astype(x_prompt.dtype), N_META)
    proj_p = in_proj(xp, 1024)
    proj_s = in_proj(xs, 512)
    proj_s3 = proj_s.reshape(db, tn, -1)

    oa_p, s_p = _gla_prompt(proj_p, proj_m, lb, hgrn_g_norm[l], nb, t)
    ob_p, k_p, v_p = _sb_prompt(proj_p, proj_m, sb_gq[l], sb_gk[l], sb_bias[l], nb, t)
    y_p = _finish(xp, oa_p, ob_p, proj_p, *lw, tm=1024)

    oa_s, s_s = _gla_sample(proj_s3, lead(state_hgrn), lb, hgrn_g_norm[l])
    ob_s, k_s, v_s = _sb_sample(proj_s3, lead(cache_k), lead(cache_v), page_table, sb_gq[l], sb_gk[l],
                                sb_bias[l])
    y_s = _finish(xs, oa_s.reshape(db * tn, HW), ob_s.reshape(db * tn, HW), proj_s, *lw, tm=512)

    return (y_p.reshape(nb, t, d), y_s.reshape(db, tn, d),
            k_p.reshape(1, nb, N_META + t, HEADS, DH), v_p.reshape(1, nb, N_META + t, HEADS, DH),
            s_p[None],
            k_s.reshape(1, db, tn, HEADS, DH).astype(cache_k.dtype),
            v_s.reshape(1, db, tn, HEADS, DH).astype(cache_v.dtype),
            s_s[None].astype(state_hgrn.dtype))
```

```python
import functools
import math

import jax
import jax.numpy as jnp
from jax import lax
from jax.experimental import pallas as pl
from jax.experimental.pallas import tpu as pltpu

N_META = 16
HEADS = 16
DH = 128
HW = HEADS * DH
EPS = 1e-6
SUBLANES = 8
BF16_ROWS = 16
GLA_CHUNK = 16
GLA_ROWS = 128
GLA_UNROLL = 8
SB_BLOCK = 128
SB_QTILE = 2048
SB_PAGES_PER_STEP = 4
EXP_CLAMP = 80.0
VMEM_LIMIT = 56 * 1024 * 1024
IN_TN = 1024
TAIL_TN = 512

COL_QA, COL_FA, COL_IA, COL_OG, COL_QB, COL_KB, COL_VB = range(7)
COL_GTA = 7 * HW

_NT = (((1,), (1,)), ((), ()))
_TN = (((0,), (0,)), ((), ()))
BF = jnp.bfloat16
F32 = jnp.float32


def _params(sem):
    return pltpu.CompilerParams(dimension_semantics=sem, vmem_limit_bytes=VMEM_LIMIT)


def _sigmoid(x):
    return 1.0 / (1.0 + jnp.exp(-x))


def _row_rmsnorm(x, g):
    return x * lax.rsqrt(jnp.mean(x * x, axis=-1, keepdims=True) + EPS) * g


def _rmsnorm_kernel(x_ref, g_ref, o_ref):
    o_ref[...] = _row_rmsnorm(x_ref[...], g_ref[...]).astype(o_ref.dtype)


def _rmsnorm_bf16(x, g, tm=256):
    m, d = x.shape
    tm = min(tm, m)
    return pl.pallas_call(
        _rmsnorm_kernel,
        grid=(pl.cdiv(m, tm),),
        in_specs=[pl.BlockSpec((tm, d), lambda i: (i, 0)), pl.BlockSpec((1, d), lambda i: (0, 0))],
        out_specs=pl.BlockSpec((tm, d), lambda i: (i, 0)),
        out_shape=jax.ShapeDtypeStruct((m, d), BF),
        compiler_params=_params(("parallel",)),
        name="rmsnorm_bf16",
    )(x, g.reshape(1, d))


def _dense_kernel(*refs, mode):
    o_ref = refs[-1]
    if mode == "plain":
        a, w = refs[:2]
        o_ref[...] = jnp.dot(a[...], w[...], preferred_element_type=F32).astype(o_ref.dtype)
    elif mode == "residual":
        a, w, r = refs[:3]
        o_ref[...] = r[...] + jnp.dot(a[...], w[...], preferred_element_type=F32)
    elif mode == "swiglu":
        a, wg, wu = refs[:3]
        g = jnp.dot(a[...], wg[...], preferred_element_type=F32)
        u = jnp.dot(a[...], wu[...], preferred_element_type=F32)
        o_ref[...] = (g * _sigmoid(g) * u).astype(o_ref.dtype)
    elif mode == "gated_sum":
        a, b, wa, wb, ga, gb = refs[:6]
        pa = jnp.dot(a[...], wa[...], preferred_element_type=F32)
        pb = jnp.dot(b[...], wb[...], preferred_element_type=F32)
        o_ref[...] = (_sigmoid(ga[...]) * pa + _sigmoid(gb[...]) * pb).astype(o_ref.dtype)
    else:
        raise ValueError(mode)


def _dense(mode, acts, weights, n, extras, extra_col_offsets, out_dtype, tm, name):
    m = acts[0].shape[0]
    tn = weights[0].shape[2]
    tm = min(tm, m)
    assert all(w.shape[0] == pl.cdiv(n, tn) and w.shape[2] == tn for w in weights)
    in_specs = [pl.BlockSpec((tm, a.shape[1]), lambda i, j: (i, 0)) for a in acts]
    in_specs += [pl.BlockSpec((None, w.shape[1], tn), lambda i, j: (j, 0, 0)) for w in weights]
    for off in extra_col_offsets:
        assert off % tn == 0
        in_specs.append(pl.BlockSpec((tm, tn), lambda i, j, o=off // tn: (i, j + o)))
    return pl.pallas_call(
        functools.partial(_dense_kernel, mode=mode),
        grid=(pl.cdiv(m, tm), pl.cdiv(n, tn)),
        in_specs=in_specs,
        out_specs=pl.BlockSpec((tm, tn), lambda i, j: (i, j)),
        out_shape=jax.ShapeDtypeStruct((m, n), out_dtype),
        compiler_params=_params(("parallel", "arbitrary")),
        name=name,
    )(*acts, *weights, *extras)


def _block_columns(w, tn):
    k, n = w.shape
    assert n % tn == 0
    return w.reshape(k, n // tn, tn).transpose(1, 0, 2)


def _side_cast_specs(side, grid, tn):
    steps = math.prod(grid)
    strides = [math.prod(grid[a + 1:]) for a in range(len(grid))]
    in_specs, out_specs, shapes = [], [], []
    for x in side:
        rows, cols = x.shape
        r = BF16_ROWS * pl.cdiv(rows, BF16_ROWS * steps)
        while rows % r and r < rows:
            r += BF16_ROWS
        last = pl.cdiv(rows, r) - 1
        tw = min(tn, cols)
        nblk = pl.cdiv(cols, tw)

        def chunk(*idx, last=last):
            return jnp.minimum(sum(i * st for i, st in zip(idx[:len(grid)], strides)), last)

        in_specs.append(pl.BlockSpec((r, cols), lambda *idx, chunk=chunk: (chunk(*idx), 0)))
        out_specs.append(pl.BlockSpec((nblk, r, tw), lambda *idx, chunk=chunk: (0, chunk(*idx), 0)))
        shapes.append(jax.ShapeDtypeStruct((nblk, rows, tw), BF))
    return in_specs, out_specs, shapes


def _side_cast(side_in, side_out):
    for src, dst in zip(side_in, side_out):
        nblk, r, tw = dst.shape
        cols = src.shape[1]
        for j in range(nblk):
            w = min(tw, cols - j * tw)
            dst[j, :, :w] = src[:, j * tw:j * tw + w].astype(dst.dtype)
            if w < tw:
                dst[j, :, w:] = jnp.zeros((r, tw - w), dst.dtype)


def _split3(x):
    hi = x.astype(BF)
    r = x - hi.astype(F32)
    mid = r.astype(BF)
    lo = (r - mid.astype(F32)).astype(BF)
    return hi, mid, lo


def _gla_masks(r, c):
    row = lax.broadcasted_iota(jnp.int32, (r, r), 0)
    col = lax.broadcasted_iota(jnp.int32, (r, r), 1)
    rid = lax.broadcasted_iota(jnp.int32, (r, 1), 0)
    levels = []
    size = r
    while size > c:
        half = size // 2
        mask = jnp.logical_and(row // size == col // size,
                               jnp.logical_and(row % size >= half, col % size < half))
        levels.append((size, rid % size >= half, mask))
        size = half
    return row >= col, levels, jnp.logical_and(row >= col, row // c == col // c)


def _gla_front(qa, fr, ia, lb, masks, c, valid=None):
    r = qa.shape[0]
    tri, levels, chunk_mask = masks
    e = jnp.exp(-jnp.abs(fr))
    inv = 1.0 / (1.0 + e)
    pos = fr >= 0
    sig = jnp.where(pos, inv, e * inv)
    nsig = jnp.where(pos, e * inv, inv)
    logf = jnp.log(lb + (1.0 - lb) * sig)
    k = (1.0 - lb) * nsig
    if valid is not None:
        logf = jnp.where(valid, logf, 0.0)
        k = jnp.where(valid, k, 0.0)
    q = qa * _sigmoid(qa)
    v = ia.astype(BF)
    tri_b = tri.astype(BF)
    hi, mid, lo = _split3(logf)
    b = (jnp.dot(tri_b, hi, preferred_element_type=F32) + jnp.dot(tri_b, mid, preferred_element_type=F32)
         + jnp.dot(tri_b, lo, preferred_element_type=F32))

    def at_row(size, i):
        b3 = b.reshape(r // size, size, DH)
        return jnp.broadcast_to(b3[:, i:i + 1, :], b3.shape).reshape(r, DH)

    bm = at_row(c, c // 2 - 1)
    qh = (q * jnp.exp(jnp.minimum(b - bm, EXP_CLAMP))).astype(BF)
    kh = (k * jnp.exp(jnp.minimum(bm - b, EXP_CLAMP))).astype(BF)
    attn = jnp.where(chunk_mask, lax.dot_general(qh, kh, _NT, preferred_element_type=F32), 0.0)
    for size, upper, mask in levels:
        bs = at_row(size, size // 2 - 1)
        x = (jnp.where(upper, q, k) * jnp.exp(jnp.where(upper, b - bs, bs - b))).astype(BF)
        attn = attn + jnp.where(mask, lax.dot_general(x, x, _NT, preferred_element_type=F32), 0.0)
    return q, k, v, b, jnp.dot(attn.astype(BF), v, preferred_element_type=F32), at_row


def _gla_rows(st, qa, fr, ia, lb, masks, c):
    r = qa.shape[0]
    q, k, v, b, o, _ = _gla_front(qa, fr, ia, lb, masks, c)
    qs = (q * jnp.exp(b)).astype(BF)
    bl = b[r - 1:r, :]
    kt = (k * jnp.exp(bl - b)).astype(BF)
    o = o + lax.dot_general(qs, st.astype(BF), _NT, preferred_element_type=F32)
    st = st * jnp.exp(bl) + lax.dot_general(v, kt, _TN, preferred_element_type=F32)
    return st, o


def _gla_out(o, og, gn):
    return (_row_rmsnorm(o, gn) * (og * _sigmoid(og))).astype(BF)


def _gla_prompt_kernel(qa_ref, fa_ref, ia_ref, og_ref, mfa_ref, mia_ref, lb_ref, gn_ref,
                       oa_ref, s_ref, *, n_iters):
    lb = lb_ref[...]
    gn = gn_ref[...]
    st, _ = _gla_rows(jnp.zeros((DH, DH), F32), jnp.zeros((N_META, DH), F32), mfa_ref[...], mia_ref[...], lb,
                      _gla_masks(N_META, GLA_CHUNK), GLA_CHUNK)
    masks = _gla_masks(GLA_ROWS, GLA_CHUNK)

    def body(i, st):
        rows = pl.ds(pl.multiple_of(i * GLA_ROWS, GLA_ROWS), GLA_ROWS)
        st, o = _gla_rows(st, qa_ref[rows, :], fa_ref[rows, :], ia_ref[rows, :], lb, masks, GLA_CHUNK)
        oa_ref[rows, :] = _gla_out(o, og_ref[rows, :], gn)
        return st

    st = lax.fori_loop(0, n_iters, body, st, unroll=GLA_UNROLL if n_iters % GLA_UNROLL == 0 else 1)
    s_ref[...] = st.T


def _gla_prompt(proj, proj_meta, lb, g_norm, nb, t):
    blk = lambda cb: pl.BlockSpec((t, DH), lambda n, h, cb=cb: (n, cb * HEADS + h))
    mblk = lambda cb: pl.BlockSpec((N_META, DH), lambda n, h, cb=cb: (0, cb * HEADS + h))
    return pl.pallas_call(
        functools.partial(_gla_prompt_kernel, n_iters=t // GLA_ROWS),
        grid=(nb, HEADS),
        in_specs=[blk(COL_QA), blk(COL_FA), blk(COL_IA), blk(COL_OG), mblk(COL_FA), mblk(COL_IA),
                  pl.BlockSpec((1, DH), lambda n, h: (0, h)), pl.BlockSpec((1, DH), lambda n, h: (0, 0))],
        out_specs=[pl.BlockSpec((t, DH), lambda n, h: (n, h)),
                   pl.BlockSpec((None, None, DH, DH), lambda n, h: (n, h, 0, 0))],
        out_shape=[jax.ShapeDtypeStruct((nb * t, HW), BF),
                   jax.ShapeDtypeStruct((nb, HEADS, DH, DH), F32)],
        compiler_params=_params(("parallel", "parallel")),
        name="gla_prompt",
    )(proj, proj, proj, proj, proj_meta, proj_meta, lb.reshape(1, HW), g_norm.reshape(1, DH))


def _gla_sample_kernel(qa_ref, fa_ref, ia_ref, og_ref, s0_ref, lb_ref, gn_ref, oa_ref, s_ref, *, tn):
    c = SUBLANES
    r = HEADS * c
    row = lax.broadcasted_iota(jnp.int32, (r, r), 0)
    col = lax.broadcasted_iota(jnp.int32, (r, r), 1)
    own = jnp.logical_and(row >= col, row // c == col // c)
    valid = lax.broadcasted_iota(jnp.int32, (r, DH), 0) % c < tn
    pad = jnp.zeros((c - tn, DH), F32)

    def stack(ref):
        return jnp.concatenate([x for h in range(HEADS) for x in (ref[:, h * DH:(h + 1) * DH], pad)], axis=0)

    lb = jnp.concatenate([jnp.broadcast_to(lb_ref[:, h * DH:(h + 1) * DH], (c, DH)) for h in range(HEADS)], axis=0)
    q, k, v, b, o, at_row = _gla_front(stack(qa_ref), stack(fa_ref), stack(ia_ref), lb, (own, [], own), c, valid)
    qs = (q * jnp.exp(b)).astype(BF)
    bl = at_row(c, c - 1)
    kt = (k * jnp.exp(bl - b)).astype(BF)
    decay = jnp.exp(bl)
    o_state = []
    for h in range(HEADS):
        rows = slice(h * c, (h + 1) * c)
        st = s0_ref[h]
        o_state.append(jnp.dot(qs[rows], st.astype(BF), preferred_element_type=F32))
        decay_col = decay[rows].T[:, 0:1]
        s_ref[h] = st * decay_col + lax.dot_general(kt[rows], v[rows], _TN, preferred_element_type=F32)
    out = _gla_out(o + jnp.concatenate(o_state, axis=0), stack(og_ref), gn_ref[...])
    for h in range(HEADS):
        oa_ref[:, h * DH:(h + 1) * DH] = out[h * c:h * c + tn, :]


def _gla_sample(proj3, state, lb, g_norm):
    db, tn, _ = proj3.shape
    assert tn <= SUBLANES
    blk = lambda cb: pl.BlockSpec((None, tn, HW), lambda b, cb=cb: (b, 0, cb))
    sblk = pl.BlockSpec((None, HEADS, DH, DH), lambda b: (b, 0, 0, 0))
    return pl.pallas_call(
        functools.partial(_gla_sample_kernel, tn=tn),
        grid=(db,),
        in_specs=[blk(COL_QA), blk(COL_FA), blk(COL_IA), blk(COL_OG), sblk,
                  pl.BlockSpec((1, HW), lambda b: (0, 0)), pl.BlockSpec((1, DH), lambda b: (0, 0))],
        out_specs=[pl.BlockSpec((None, tn, HW), lambda b: (b, 0, 0)), sblk],
        out_shape=[jax.ShapeDtypeStruct((db, tn, HW), BF),
                   jax.ShapeDtypeStruct((db, HEADS, DH, DH), F32)],
        compiler_params=_params(("parallel",)),
        name="gla_sample",
    )(proj3, proj3, proj3, proj3, state, lb.reshape(1, HW), g_norm.reshape(1, DH))


def _sb_tail(z, w2, mask):
    kk = z.shape[1]
    log_beta = jnp.minimum(z, 0.0) - jnp.log(1.0 + jnp.exp(-jnp.abs(z)))
    l1m = log_beta - z
    if mask is not None:
        l1m = jnp.where(mask, l1m, 0.0)
    hi = l1m.astype(BF)
    lo = (l1m - hi.astype(F32)).astype(BF)
    r = jnp.dot(jnp.concatenate([hi, lo], axis=1), w2, preferred_element_type=F32)
    return log_beta + r[:, :kk], r[:, kk:]


def _sb_exp(logw, c, mask):
    a = jnp.exp(logw + c)
    return a if mask is None else jnp.where(mask, a, 0.0)


def _sb_w2(kk):
    row = lax.broadcasted_iota(jnp.int32, (2 * kk, 2 * kk), 0) % kk
    col = lax.broadcasted_iota(jnp.int32, (2 * kk, 2 * kk), 1)
    return jnp.logical_or(col >= kk, row > col).astype(BF)


def _sb_prompt_kernel(qb_ref, kb_ref, vb_ref, mk_ref, mv_ref, gq_ref, gk_ref, bias_ref, *refs, qtile, n_side):
    side_in, (ob_ref, ko_ref, vo_ref), side_out = refs[:n_side], refs[n_side:n_side + 3], refs[n_side + 3:2 * n_side + 3]
    q_s, k_s, v_s, mk_s, mv_s, w_s, c_s, acc_s = refs[2 * n_side + 3:]
    _side_cast(side_in, side_out)
    it = pl.program_id(2)
    blk = SB_BLOCK
    nsub = qtile // blk
    scale = 1.0 / math.sqrt(DH)

    @pl.when(it == 0)
    def _():
        kn = _row_rmsnorm(kb_ref[...], gk_ref[...])
        ko_ref[N_META:, :] = kn
        k_s[...] = kn.astype(BF)
        mkn = _row_rmsnorm(mk_ref[...], gk_ref[...])
        ko_ref[:N_META, :] = mkn
        mk_s[...] = jnp.zeros(mk_s.shape, BF)
        mk_s[:N_META, :] = mkn.astype(BF)
        v = vb_ref[...]
        vo_ref[N_META:, :] = v
        v_s[...] = v.astype(BF)
        mv = mv_ref[...]
        vo_ref[:N_META, :] = mv
        mv_s[...] = jnp.zeros(mv_s.shape, BF)
        mv_s[:N_META, :] = mv.astype(BF)
        q_s[...] = _row_rmsnorm(qb_ref[...], gq_ref[...]).astype(BF)
        w_s[...] = _sb_w2(blk)

    bias = bias_ref[...]
    w2 = w_s[...]
    c_s[...] = jnp.zeros(c_s.shape, F32)
    acc_s[...] = jnp.zeros(acc_s.shape, F32)

    def sweep(r0, kj, vj, visible):
        nr = qtile - r0
        rows = pl.ds(r0, nr)
        qi = q_s[pl.ds(pl.multiple_of(it * qtile, qtile) + r0, nr), :]
        z = lax.dot_general(qi, kj, _NT, preferred_element_type=F32) * scale + bias
        col = lax.broadcasted_iota(jnp.int32, (nr, blk), 1)
        if visible is None:
            m = None
        elif visible == "causal":
            m = col < lax.broadcasted_iota(jnp.int32, (nr, blk), 0)
        else:
            m = col < visible
        logw, tot = _sb_tail(z, w2, m)
        c = c_s[rows, :]
        a = _sb_exp(logw, c, m)
        c_s[rows, :] = c + tot
        acc_s[rows, :] += jnp.dot(a.astype(BF), vj, preferred_element_type=F32)

    def key_rows(j):
        return pl.ds(pl.multiple_of(j * blk, blk), blk)

    for s in reversed(range(nsub)):
        j = it * nsub + s
        sweep(s * blk, k_s[key_rows(j), :], v_s[key_rows(j), :], "causal")

    def body(jj, carry):
        for u in range(nsub):
            j = (it - jj) * nsub - 1 - u
            sweep(0, k_s[key_rows(j), :], v_s[key_rows(j), :], None)
        return carry

    lax.fori_loop(0, it, body, 0)
    sweep(0, mk_s[...], mv_s[...], N_META)
    ob_ref[...] = acc_s[...].astype(ob_ref.dtype)


def _sb_prompt(proj, proj_meta, gq, gk, bias, nb, t, side):
    qtile = min(SB_QTILE, t)
    assert t % qtile == 0 and qtile % SB_BLOCK == 0
    nq = t // qtile
    side_in_specs, side_out_specs, side_shapes = _side_cast_specs(side, (nb, HEADS, nq), TAIL_TN)
    blk = lambda cb: pl.BlockSpec((t, DH), lambda n, h, i, cb=cb: (n, cb * HEADS + h))
    mblk = lambda cb: pl.BlockSpec((N_META, DH), lambda n, h, i, cb=cb: (0, cb * HEADS + h))
    vec = pl.BlockSpec((1, DH), lambda n, h, i: (0, 0))
    kv_out = pl.BlockSpec((None, N_META + t, DH), lambda n, h, i: (n, 0, h))
    bias_b = jnp.broadcast_to(bias.astype(F32).reshape(HEADS, 1, 1), (HEADS, 1, SB_BLOCK))
    out = pl.pallas_call(
        functools.partial(_sb_prompt_kernel, qtile=qtile, n_side=len(side)),
        grid=(nb, HEADS, nq),
        in_specs=[blk(COL_QB), blk(COL_KB), blk(COL_VB), mblk(COL_KB), mblk(COL_VB), vec, vec,
                  pl.BlockSpec((None, 1, SB_BLOCK), lambda n, h, i: (h, 0, 0))] + side_in_specs,
        out_specs=[pl.BlockSpec((qtile, DH), lambda n, h, i: (n * nq + i, h)), kv_out, kv_out] + side_out_specs,
        out_shape=[jax.ShapeDtypeStruct((nb * t, HW), BF),
                   jax.ShapeDtypeStruct((nb, N_META + t, HW), F32),
                   jax.ShapeDtypeStruct((nb, N_META + t, HW), F32)] + side_shapes,
        scratch_shapes=[pltpu.VMEM((t, DH), BF), pltpu.VMEM((t, DH), BF), pltpu.VMEM((t, DH), BF),
                        pltpu.VMEM((SB_BLOCK, DH), BF), pltpu.VMEM((SB_BLOCK, DH), BF),
                        pltpu.VMEM((2 * SB_BLOCK, 2 * SB_BLOCK), BF),
                        pltpu.VMEM((qtile, SB_BLOCK), F32), pltpu.VMEM((qtile, DH), F32)],
        compiler_params=_params(("arbitrary", "arbitrary", "arbitrary")),
        name="sb_prompt",
    )(proj, proj, proj, proj_meta, proj_meta, gq.reshape(1, DH), gk.reshape(1, DH), bias_b, *side)
    return out[0], out[1], out[2], out[3:]


def _sb_sample_kernel(pt_ref, qb_ref, kb_ref, vb_ref, *refs, tn, n_steps, pages_per_step):
    page_refs = refs[:4 * pages_per_step]
    gq_ref, gk_ref, bias_ref, ob_ref, ko_ref, vo_ref, q_s, c_s, acc_s, w_s = refs[4 * pages_per_step:]
    p = pl.program_id(1)
    rq = 16
    ro = SUBLANES
    half = HEADS // 2
    pg = page_refs[0].shape[0]
    scale = 1.0 / math.sqrt(DH)
    zrows = jnp.zeros((rq - ro, pg), F32)

    def sweep(blocks, mask):
        nblk = len(blocks)
        hr = HEADS * ro
        z = jnp.concatenate(
            [lax.dot_general(q_s[h * rq:(h + 1) * rq, :], get_k(h), _NT, preferred_element_type=F32)[:ro]
             for get_k, _ in blocks for h in range(HEADS)], axis=0)
        z = (z.reshape(nblk, hr, pg) * scale + bias_ref[...]).reshape(nblk * hr, pg)
        c0 = c_s[...]
        logw, tot = _sb_tail(z, w_s[...], mask)
        cs = []
        for s in range(nblk):
            cs.append(c0)
            c0 = c0 + tot[s * hr:(s + 1) * hr]
        c_s[...] = c0
        a = _sb_exp(logw, jnp.concatenate(cs, axis=0), mask)
        acc = acc_s[...]
        for s, (_, get_v) in enumerate(blocks):
            acc = acc + jnp.concatenate(
                [jnp.dot(jnp.concatenate([a[s * hr + h * ro:s * hr + (h + 1) * ro], zrows], axis=0).astype(BF),
                         get_v(h), preferred_element_type=F32)[:ro] for h in range(HEADS)], axis=0)
        acc_s[...] = acc

    @pl.when(p == 0)
    def _():
        w_s[...] = _sb_w2(pg)
        c_s[...] = jnp.zeros(c_s.shape, F32)
        acc_s[...] = jnp.zeros(acc_s.shape, F32)
        q_s[...] = jnp.zeros(q_s.shape, BF)
        vo_ref[...] = vb_ref[...]
        kpad = jnp.zeros((pg - tn, DH), BF)
        ks, vs = [], []
        for h in range(HEADS):
            cols = slice(h * DH, (h + 1) * DH)
            q_s[h * rq:h * rq + tn, :] = _row_rmsnorm(qb_ref[:, cols], gq_ref[...]).astype(BF)
            kn = _row_rmsnorm(kb_ref[:, cols], gk_ref[...])
            ko_ref[:, cols] = kn
            ks.append(jnp.concatenate([kn.astype(BF), kpad], axis=0))
            vs.append(jnp.concatenate([vb_ref[:, cols].astype(BF), kpad], axis=0))
        row = lax.broadcasted_iota(jnp.int32, (HEADS * ro, pg), 0) % ro
        col = lax.broadcasted_iota(jnp.int32, (HEADS * ro, pg), 1)
        sweep([(lambda h: ks[h], lambda h: vs[h])], jnp.logical_and(col < row, col < tn))

    def head(lo, hi, h):
        return (lo if h < half else hi)[pl.ds(h % half, pg, stride=half), :].astype(BF)

    blocks = []
    for s in range(pages_per_step):
        k_lo, k_hi, v_lo, v_hi = [r.reshape(pg * half, DH) for r in page_refs[4 * s:4 * s + 4]]
        blocks.append((functools.partial(head, k_lo, k_hi), functools.partial(head, v_lo, v_hi)))
    sweep(blocks, None)

    @pl.when(p == n_steps - 1)
    def _():
        for h in range(HEADS):
            ob_ref[:, h * DH:(h + 1) * DH] = acc_s[h * ro:h * ro + tn, :].astype(ob_ref.dtype)


def _sb_sample(proj3, cache_k, cache_v, page_table, gq, gk, bias):
    db, tn, _ = proj3.shape
    n_pages = page_table.shape[1]
    pg = cache_k.shape[1]
    pps = SB_PAGES_PER_STEP if n_pages % SB_PAGES_PER_STEP == 0 else 1
    n_steps = n_pages // pps
    half = HEADS // 2
    assert tn <= SUBLANES and half == SUBLANES
    blk = lambda cb: pl.BlockSpec((None, tn, HW), lambda b, p, pt, cb=cb: (b, 0, cb))

    def page(s, g):
        return pl.BlockSpec((None, pg, half, DH),
                            lambda b, p, pt, s=s, g=g: (pt[b, n_pages - 1 - (p * pps + s)], 0, g, 0))

    pages, page_specs = [], []
    for s in range(pps):
        pages += [cache_k, cache_k, cache_v, cache_v]
        page_specs += [page(s, 0), page(s, 1), page(s, 0), page(s, 1)]
    vec = pl.BlockSpec((1, DH), lambda b, p, pt: (0, 0))
    out = pl.BlockSpec((None, tn, HW), lambda b, p, pt: (b, 0, 0))
    bias_b = jnp.broadcast_to(bias.astype(F32).reshape(HEADS, 1, 1), (HEADS, SUBLANES, pg)).reshape(HEADS * SUBLANES, pg)
    return pl.pallas_call(
        functools.partial(_sb_sample_kernel, tn=tn, n_steps=n_steps, pages_per_step=pps),
        grid_spec=pltpu.PrefetchScalarGridSpec(
            num_scalar_prefetch=1,
            grid=(db, n_steps),
            in_specs=[blk(COL_QB), blk(COL_KB), blk(COL_VB)] + page_specs + [
                vec, vec, pl.BlockSpec((HEADS * SUBLANES, pg), lambda b, p, pt: (0, 0))],
            out_specs=[out, out, out],
            scratch_shapes=[pltpu.VMEM((HEADS * 16, DH), BF), pltpu.VMEM((HEADS * SUBLANES, pg), F32),
                            pltpu.VMEM((HEADS * SUBLANES, DH), F32), pltpu.VMEM((2 * pg, 2 * pg), BF)]),
        out_shape=[jax.ShapeDtypeStruct((db, tn, HW), BF),
                   jax.ShapeDtypeStruct((db, tn, HW), F32),
                   jax.ShapeDtypeStruct((db, tn, HW), F32)],
        compiler_params=_params(("parallel", "arbitrary")),
        name="sb_sample",
    )(page_table, proj3, proj3, proj3, *pages, gq.reshape(1, DH), gk.reshape(1, DH), bias_b)


def _finish(x, oa, ob, proj, w_pa, w_pb, w_o, g_ffn, w_gate, w_up, w_down, d_ff, tm):
    d = x.shape[1]
    merged = _dense("gated_sum", [oa, ob], [w_pa, w_pb], d, [proj, proj], [COL_GTA, COL_GTA + d], BF, tm,
                    "merge_proj")
    x1 = _dense("residual", [merged], [w_o], d, [x], [0], F32, tm, "out_proj")
    h = _rmsnorm_bf16(x1, g_ffn)
    act = _dense("swiglu", [h], [w_gate, w_up], d_ff, [], [], BF, tm, "ffn_up")
    return _dense("residual", [act], [w_down], d, [x1], [0], F32, min(tm, 512), "ffn_down")


def kernel(x_prompt, x_sample, cache_k, cache_v, state_hgrn, page_table, meta_tokens, g_mix, w_in,
           hgrn_lb_logits, hgrn_g_norm, sb_gq, sb_gk, sb_bias, w_pa, w_pb, w_o, g_ffn, w_gate, w_up, w_down):
    nb, t, d = x_prompt.shape
    db, tn, _ = x_sample.shape
    depth = w_in.shape[0]
    assert depth == 1, "single-layer trunk"
    assert t % GLA_ROWS == 0 and N_META % GLA_CHUNK == 0
    l = 0
    lb = jnp.cumsum(jax.nn.softmax(hgrn_lb_logits.astype(F32), axis=0), axis=0)[l]
    lead = lambda a: a.reshape(a.shape[1:])
    in_width = w_in.shape[-1]
    in_tn = IN_TN
    while in_width % in_tn:
        in_tn //= 2
    w_in_b = _block_columns(lead(w_in).astype(BF), in_tn)

    xp = x_prompt.reshape(nb * t, d)
    xs = x_sample.reshape(db * tn, d)

    def in_proj(x, tm):
        return _dense("plain", [_rmsnorm_bf16(x, g_mix[l])], [w_in_b], in_width, [], [], F32, tm, "in_proj")

    proj_m = in_proj(meta_tokens.astype(x_prompt.dtype), N_META)
    proj_p = in_proj(xp, 1024)
    proj_s = in_proj(xs, 512)
    proj_s3 = proj_s.reshape(db, tn, -1)

    oa_p, s_p = _gla_prompt(proj_p, proj_m, lb, hgrn_g_norm[l], nb, t)
    ob_p, k_p, v_p, (w_pa_b, w_pb_b, w_o_b, w_gate_b, w_up_b, w_down_b) = _sb_prompt(
        proj_p, proj_m, sb_gq[l], sb_gk[l], sb_bias[l], nb, t,
        [lead(w) for w in (w_pa, w_pb, w_o, w_gate, w_up, w_down)])
    lw = (w_pa_b, w_pb_b, w_o_b, g_ffn[l], w_gate_b, w_up_b, w_down_b)
    d_ff = w_gate.shape[-1]
    y_p = _finish(xp, oa_p, ob_p, proj_p, *lw, d_ff, tm=1024)

    oa_s, s_s = _gla_sample(proj_s3, lead(state_hgrn), lb, hgrn_g_norm[l])
    ob_s, k_s, v_s = _sb_sample(proj_s3, lead(cache_k), lead(cache_v), page_table, sb_gq[l], sb_gk[l],
                                sb_bias[l])
    y_s = _finish(xs, oa_s.reshape(db * tn, HW), ob_s.reshape(db * tn, HW), proj_s, *lw, d_ff, tm=512)

    return (y_p.reshape(nb, t, d), y_s.reshape(db, tn, d),
            k_p.reshape(1, nb, N_META + t, HEADS, DH), v_p.reshape(1, nb, N_META + t, HEADS, DH),
            s_p[None],
            k_s.reshape(1, db, tn, HEADS, DH).astype(cache_k.dtype),
            v_s.reshape(1, db, tn, HEADS, DH).astype(cache_v.dtype),
            s_s[None].astype(state_hgrn.dtype))
```

```python
import functools
import math

import jax
import jax.numpy as jnp
from jax import lax
from jax.experimental import pallas as pl
from jax.experimental.pallas import tpu as pltpu

N_META = 16
HEADS = 16
DH = 128
HW = HEADS * DH
EPS = 1e-6
SUBLANES = 8
BF16_ROWS = 16
GLA_CHUNK = 16
GLA_ROWS = 128
GLA_UNROLL = 8
SB_BLOCK = 128
SB_QTILE = 2048
SB_PAGES_PER_STEP = 4
EXP_CLAMP = 80.0
VMEM_LIMIT = 56 * 1024 * 1024
IN_TN = 1024
TAIL_TN = 512

COL_QA, COL_FA, COL_IA, COL_OG, COL_QB, COL_KB, COL_VB = range(7)
COL_GTA = 7 * HW

_NT = (((1,), (1,)), ((), ()))
_TN = (((0,), (0,)), ((), ()))
BF = jnp.bfloat16
F32 = jnp.float32


def _params(sem):
    return pltpu.CompilerParams(dimension_semantics=sem, vmem_limit_bytes=VMEM_LIMIT)


def _sigmoid(x):
    return 1.0 / (1.0 + jnp.exp(-x))


def _row_rmsnorm(x, g):
    return x * lax.rsqrt(jnp.mean(x * x, axis=-1, keepdims=True) + EPS) * g


def _rmsnorm_kernel(x_ref, g_ref, o_ref):
    o_ref[...] = _row_rmsnorm(x_ref[...], g_ref[...]).astype(o_ref.dtype)


def _rmsnorm_bf16(x, g, tm=256):
    m, d = x.shape
    tm = min(tm, m)
    return pl.pallas_call(
        _rmsnorm_kernel,
        grid=(pl.cdiv(m, tm),),
        in_specs=[pl.BlockSpec((tm, d), lambda i: (i, 0)), pl.BlockSpec((1, d), lambda i: (0, 0))],
        out_specs=pl.BlockSpec((tm, d), lambda i: (i, 0)),
        out_shape=jax.ShapeDtypeStruct((m, d), BF),
        compiler_params=_params(("parallel",)),
        name="rmsnorm_bf16",
    )(x, g.reshape(1, d))


def _dense_kernel(*refs, mode):
    o_ref = refs[-1]
    if mode == "plain":
        a, w = refs[:2]
        o_ref[...] = jnp.dot(a[...], w[...], preferred_element_type=F32).astype(o_ref.dtype)
    elif mode == "residual":
        a, w, r = refs[:3]
        o_ref[...] = r[...] + jnp.dot(a[...], w[...], preferred_element_type=F32)
    elif mode == "swiglu":
        a, wg, wu = refs[:3]
        g = jnp.dot(a[...], wg[...], preferred_element_type=F32)
        u = jnp.dot(a[...], wu[...], preferred_element_type=F32)
        o_ref[...] = (g * _sigmoid(g) * u).astype(o_ref.dtype)
    elif mode == "gated_sum":
        a, b, wa, wb, ga, gb = refs[:6]
        pa = jnp.dot(a[...], wa[...], preferred_element_type=F32)
        pb = jnp.dot(b[...], wb[...], preferred_element_type=F32)
        o_ref[...] = (_sigmoid(ga[...]) * pa + _sigmoid(gb[...]) * pb).astype(o_ref.dtype)
    else:
        raise ValueError(mode)


def _dense(mode, acts, weights, n, extras, extra_col_offsets, out_dtype, tm, name):
    m = acts[0].shape[0]
    tm = min(tm, m)
    in_specs = [pl.BlockSpec((tm, a.shape[1]), lambda i, j: (i, 0)) for a in acts]
    if weights[0].ndim == 3:
        tn = weights[0].shape[2]
        assert all(w.shape[0] == pl.cdiv(n, tn) and w.shape[2] == tn for w in weights)
        in_specs += [pl.BlockSpec((None, w.shape[1], tn), lambda i, j: (j, 0, 0)) for w in weights]
    else:
        tn = min(IN_TN, n)
        assert all(w.shape[1] == n for w in weights)
        in_specs += [pl.BlockSpec((w.shape[0], tn), lambda i, j: (0, j)) for w in weights]
    for off in extra_col_offsets:
        assert off % tn == 0
        in_specs.append(pl.BlockSpec((tm, tn), lambda i, j, o=off // tn: (i, j + o)))
    return pl.pallas_call(
        functools.partial(_dense_kernel, mode=mode),
        grid=(pl.cdiv(m, tm), pl.cdiv(n, tn)),
        in_specs=in_specs,
        out_specs=pl.BlockSpec((tm, tn), lambda i, j: (i, j)),
        out_shape=jax.ShapeDtypeStruct((m, n), out_dtype),
        compiler_params=_params(("parallel", "arbitrary")),
        name=name,
    )(*acts, *weights, *extras)


def _side_cast_specs(side, grid, tn):
    steps = math.prod(grid)
    strides = [math.prod(grid[a + 1:]) for a in range(len(grid))]
    in_specs, out_specs, shapes = [], [], []
    for x in side:
        rows, cols = x.shape
        r = BF16_ROWS * pl.cdiv(rows, BF16_ROWS * steps)
        while rows % r and r < rows:
            r += BF16_ROWS
        last = pl.cdiv(rows, r) - 1
        tw = min(tn, cols)
        nblk = pl.cdiv(cols, tw)

        def chunk(*idx, last=last):
            return jnp.minimum(sum(i * st for i, st in zip(idx[:len(grid)], strides)), last)

        in_specs.append(pl.BlockSpec((r, cols), lambda *idx, chunk=chunk: (chunk(*idx), 0)))
        out_specs.append(pl.BlockSpec((nblk, r, tw), lambda *idx, chunk=chunk: (0, chunk(*idx), 0)))
        shapes.append(jax.ShapeDtypeStruct((nblk, rows, tw), BF))
    return in_specs, out_specs, shapes


def _side_cast(side_in, side_out):
    for src, dst in zip(side_in, side_out):
        nblk, r, tw = dst.shape
        cols = src.shape[1]
        for j in range(nblk):
            w = min(tw, cols - j * tw)
            dst[j, :, :w] = src[:, j * tw:j * tw + w].astype(dst.dtype)
            if w < tw:
                dst[j, :, w:] = jnp.zeros((r, tw - w), dst.dtype)


def _split3(x):
    hi = x.astype(BF)
    r = x - hi.astype(F32)
    mid = r.astype(BF)
    lo = (r - mid.astype(F32)).astype(BF)
    return hi, mid, lo


def _gla_masks(r, c):
    row = lax.broadcasted_iota(jnp.int32, (r, r), 0)
    col = lax.broadcasted_iota(jnp.int32, (r, r), 1)
    rid = lax.broadcasted_iota(jnp.int32, (r, 1), 0)
    levels = []
    size = r
    while size > c:
        half = size // 2
        mask = jnp.logical_and(row // size == col // size,
                               jnp.logical_and(row % size >= half, col % size < half))
        levels.append((size, rid % size >= half, mask))
        size = half
    return row >= col, levels, jnp.logical_and(row >= col, row // c == col // c)


def _gla_front(qa, fr, ia, lb, masks, c, valid=None):
    r = qa.shape[0]
    tri, levels, chunk_mask = masks
    e = jnp.exp(-jnp.abs(fr))
    inv = 1.0 / (1.0 + e)
    pos = fr >= 0
    sig = jnp.where(pos, inv, e * inv)
    nsig = jnp.where(pos, e * inv, inv)
    logf = jnp.log(lb + (1.0 - lb) * sig)
    k = (1.0 - lb) * nsig
    if valid is not None:
        logf = jnp.where(valid, logf, 0.0)
        k = jnp.where(valid, k, 0.0)
    q = qa * _sigmoid(qa)
    v = ia.astype(BF)
    tri_b = tri.astype(BF)
    hi, mid, lo = _split3(logf)
    b = (jnp.dot(tri_b, hi, preferred_element_type=F32) + jnp.dot(tri_b, mid, preferred_element_type=F32)
         + jnp.dot(tri_b, lo, preferred_element_type=F32))

    def at_row(size, i):
        b3 = b.reshape(r // size, size, DH)
        return jnp.broadcast_to(b3[:, i:i + 1, :], b3.shape).reshape(r, DH)

    bm = at_row(c, c // 2 - 1)
    qh = (q * jnp.exp(jnp.minimum(b - bm, EXP_CLAMP))).astype(BF)
    kh = (k * jnp.exp(jnp.minimum(bm - b, EXP_CLAMP))).astype(BF)
    attn = jnp.where(chunk_mask, lax.dot_general(qh, kh, _NT, preferred_element_type=F32), 0.0)
    for size, upper, mask in levels:
        bs = at_row(size, size // 2 - 1)
        x = (jnp.where(upper, q, k) * jnp.exp(jnp.where(upper, b - bs, bs - b))).astype(BF)
        attn = attn + jnp.where(mask, lax.dot_general(x, x, _NT, preferred_element_type=F32), 0.0)
    return q, k, v, b, jnp.dot(attn.astype(BF), v, preferred_element_type=F32), at_row


def _gla_rows(st, qa, fr, ia, lb, masks, c):
    r = qa.shape[0]
    q, k, v, b, o, _ = _gla_front(qa, fr, ia, lb, masks, c)
    qs = (q * jnp.exp(b)).astype(BF)
    bl = b[r - 1:r, :]
    kt = (k * jnp.exp(bl - b)).astype(BF)
    o = o + lax.dot_general(qs, st.astype(BF), _NT, preferred_element_type=F32)
    st = st * jnp.exp(bl) + lax.dot_general(v, kt, _TN, preferred_element_type=F32)
    return st, o


def _gla_out(o, og, gn):
    return (_row_rmsnorm(o, gn) * (og * _sigmoid(og))).astype(BF)


def _gla_prompt_kernel(qa_ref, fa_ref, ia_ref, og_ref, mfa_ref, mia_ref, lb_ref, gn_ref,
                       oa_ref, s_ref, *, n_iters):
    lb = lb_ref[...]
    gn = gn_ref[...]
    st, _ = _gla_rows(jnp.zeros((DH, DH), F32), jnp.zeros((N_META, DH), F32), mfa_ref[...], mia_ref[...], lb,
                      _gla_masks(N_META, GLA_CHUNK), GLA_CHUNK)
    masks = _gla_masks(GLA_ROWS, GLA_CHUNK)

    def body(i, st):
        rows = pl.ds(pl.multiple_of(i * GLA_ROWS, GLA_ROWS), GLA_ROWS)
        st, o = _gla_rows(st, qa_ref[rows, :], fa_ref[rows, :], ia_ref[rows, :], lb, masks, GLA_CHUNK)
        oa_ref[rows, :] = _gla_out(o, og_ref[rows, :], gn)
        return st

    st = lax.fori_loop(0, n_iters, body, st, unroll=GLA_UNROLL if n_iters % GLA_UNROLL == 0 else 1)
    s_ref[...] = st.T


def _gla_prompt(proj, proj_meta, lb, g_norm, nb, t):
    blk = lambda cb: pl.BlockSpec((t, DH), lambda n, h, cb=cb: (n, cb * HEADS + h))
    mblk = lambda cb: pl.BlockSpec((N_META, DH), lambda n, h, cb=cb: (0, cb * HEADS + h))
    return pl.pallas_call(
        functools.partial(_gla_prompt_kernel, n_iters=t // GLA_ROWS),
        grid=(nb, HEADS),
        in_specs=[blk(COL_QA), blk(COL_FA), blk(COL_IA), blk(COL_OG), mblk(COL_FA), mblk(COL_IA),
                  pl.BlockSpec((1, DH), lambda n, h: (0, h)), pl.BlockSpec((1, DH), lambda n, h: (0, 0))],
        out_specs=[pl.BlockSpec((t, DH), lambda n, h: (n, h)),
                   pl.BlockSpec((None, None, DH, DH), lambda n, h: (n, h, 0, 0))],
        out_shape=[jax.ShapeDtypeStruct((nb * t, HW), BF),
                   jax.ShapeDtypeStruct((nb, HEADS, DH, DH), F32)],
        compiler_params=_params(("parallel", "parallel")),
        name="gla_prompt",
    )(proj, proj, proj, proj, proj_meta, proj_meta, lb.reshape(1, HW), g_norm.reshape(1, DH))


def _gla_sample_kernel(qa_ref, fa_ref, ia_ref, og_ref, s0_ref, lb_ref, gn_ref, oa_ref, s_ref, *, tn):
    c = SUBLANES
    r = HEADS * c
    row = lax.broadcasted_iota(jnp.int32, (r, r), 0)
    col = lax.broadcasted_iota(jnp.int32, (r, r), 1)
    own = jnp.logical_and(row >= col, row // c == col // c)
    valid = lax.broadcasted_iota(jnp.int32, (r, DH), 0) % c < tn
    pad = jnp.zeros((c - tn, DH), F32)

    def stack(ref):
        return jnp.concatenate([x for h in range(HEADS) for x in (ref[:, h * DH:(h + 1) * DH], pad)], axis=0)

    lb = jnp.concatenate([jnp.broadcast_to(lb_ref[:, h * DH:(h + 1) * DH], (c, DH)) for h in range(HEADS)], axis=0)
    q, k, v, b, o, at_row = _gla_front(stack(qa_ref), stack(fa_ref), stack(ia_ref), lb, (own, [], own), c, valid)
    qs = (q * jnp.exp(b)).astype(BF)
    bl = at_row(c, c - 1)
    kt = (k * jnp.exp(bl - b)).astype(BF)
    decay = jnp.exp(bl)
    o_state = []
    for h in range(HEADS):
        rows = slice(h * c, (h + 1) * c)
        st = s0_ref[h]
        o_state.append(jnp.dot(qs[rows], st.astype(BF), preferred_element_type=F32))
        decay_col = decay[rows].T[:, 0:1]
        s_ref[h] = st * decay_col + lax.dot_general(kt[rows], v[rows], _TN, preferred_element_type=F32)
    out = _gla_out(o + jnp.concatenate(o_state, axis=0), stack(og_ref), gn_ref[...])
    for h in range(HEADS):
        oa_ref[:, h * DH:(h + 1) * DH] = out[h * c:h * c + tn, :]


def _gla_sample(proj3, state, lb, g_norm):
    db, tn, _ = proj3.shape
    assert tn <= SUBLANES
    blk = lambda cb: pl.BlockSpec((None, tn, HW), lambda b, cb=cb: (b, 0, cb))
    sblk = pl.BlockSpec((None, HEADS, DH, DH), lambda b: (b, 0, 0, 0))
    return pl.pallas_call(
        functools.partial(_gla_sample_kernel, tn=tn),
        grid=(db,),
        in_specs=[blk(COL_QA), blk(COL_FA), blk(COL_IA), blk(COL_OG), sblk,
                  pl.BlockSpec((1, HW), lambda b: (0, 0)), pl.BlockSpec((1, DH), lambda b: (0, 0))],
        out_specs=[pl.BlockSpec((None, tn, HW), lambda b: (b, 0, 0)), sblk],
        out_shape=[jax.ShapeDtypeStruct((db, tn, HW), BF),
                   jax.ShapeDtypeStruct((db, HEADS, DH, DH), F32)],
        compiler_params=_params(("parallel",)),
        name="gla_sample",
    )(proj3, proj3, proj3, proj3, state, lb.reshape(1, HW), g_norm.reshape(1, DH))


def _sb_tail(z, w2, mask):
    kk = z.shape[1]
    log_beta = jnp.minimum(z, 0.0) - jnp.log(1.0 + jnp.exp(-jnp.abs(z)))
    l1m = log_beta - z
    if mask is not None:
        l1m = jnp.where(mask, l1m, 0.0)
    hi = l1m.astype(BF)
    lo = (l1m - hi.astype(F32)).astype(BF)
    r = jnp.dot(jnp.concatenate([hi, lo], axis=1), w2, preferred_element_type=F32)
    return log_beta + r[:, :kk], r[:, kk:]


def _sb_exp(logw, c, mask):
    a = jnp.exp(logw + c)
    return a if mask is None else jnp.where(mask, a, 0.0)


def _sb_w2(kk):
    row = lax.broadcasted_iota(jnp.int32, (2 * kk, 2 * kk), 0) % kk
    col = lax.broadcasted_iota(jnp.int32, (2 * kk, 2 * kk), 1)
    return jnp.logical_or(col >= kk, row > col).astype(BF)


def _sb_prompt_kernel(qb_ref, kb_ref, vb_ref, mk_ref, mv_ref, gq_ref, gk_ref, bias_ref, *refs, qtile, n_side):
    side_in, (ob_ref, ko_ref, vo_ref), side_out = refs[:n_side], refs[n_side:n_side + 3], refs[n_side + 3:2 * n_side + 3]
    q_s, k_s, v_s, mk_s, mv_s, w_s, c_s, acc_s = refs[2 * n_side + 3:]
    _side_cast(side_in, side_out)
    it = pl.program_id(2)
    blk = SB_BLOCK
    nsub = qtile // blk
    scale = 1.0 / math.sqrt(DH)

    @pl.when(it == 0)
    def _():
        kn = _row_rmsnorm(kb_ref[...], gk_ref[...])
        ko_ref[N_META:, :] = kn
        k_s[...] = kn.astype(BF)
        mkn = _row_rmsnorm(mk_ref[...], gk_ref[...])
        ko_ref[:N_META, :] = mkn
        mk_s[...] = jnp.zeros(mk_s.shape, BF)
        mk_s[:N_META, :] = mkn.astype(BF)
        v = vb_ref[...]
        vo_ref[N_META:, :] = v
        v_s[...] = v.astype(BF)
        mv = mv_ref[...]
        vo_ref[:N_META, :] = mv
        mv_s[...] = jnp.zeros(mv_s.shape, BF)
        mv_s[:N_META, :] = mv.astype(BF)
        q_s[...] = _row_rmsnorm(qb_ref[...], gq_ref[...]).astype(BF)
        w_s[...] = _sb_w2(blk)

    bias = bias_ref[...]
    w2 = w_s[...]
    c_s[...] = jnp.zeros(c_s.shape, F32)
    acc_s[...] = jnp.zeros(acc_s.shape, F32)

    def sweep(r0, kj, vj, visible):
        nr = qtile - r0
        rows = pl.ds(r0, nr)
        qi = q_s[pl.ds(pl.multiple_of(it * qtile, qtile) + r0, nr), :]
        z = lax.dot_general(qi, kj, _NT, preferred_element_type=F32) * scale + bias
        col = lax.broadcasted_iota(jnp.int32, (nr, blk), 1)
        if visible is None:
            m = None
        elif visible == "causal":
            m = col < lax.broadcasted_iota(jnp.int32, (nr, blk), 0)
        else:
            m = col < visible
        logw, tot = _sb_tail(z, w2, m)
        c = c_s[rows, :]
        a = _sb_exp(logw, c, m)
        c_s[rows, :] = c + tot
        acc_s[rows, :] += jnp.dot(a.astype(BF), vj, preferred_element_type=F32)

    def key_rows(j):
        return pl.ds(pl.multiple_of(j * blk, blk), blk)

    for s in reversed(range(nsub)):
        j = it * nsub + s
        sweep(s * blk, k_s[key_rows(j), :], v_s[key_rows(j), :], "causal")

    def body(jj, carry):
        for u in range(nsub):
            j = (it - jj) * nsub - 1 - u
            sweep(0, k_s[key_rows(j), :], v_s[key_rows(j), :], None)
        return carry

    lax.fori_loop(0, it, body, 0)
    sweep(0, mk_s[...], mv_s[...], N_META)
    ob_ref[...] = acc_s[...].astype(ob_ref.dtype)


def _sb_prompt(proj, proj_meta, gq, gk, bias, nb, t, side):
    qtile = min(SB_QTILE, t)
    assert t % qtile == 0 and qtile % SB_BLOCK == 0
    nq = t // qtile
    side_in_specs, side_out_specs, side_shapes = _side_cast_specs(side, (nb, HEADS, nq), TAIL_TN)
    blk = lambda cb: pl.BlockSpec((t, DH), lambda n, h, i, cb=cb: (n, cb * HEADS + h))
    mblk = lambda cb: pl.BlockSpec((N_META, DH), lambda n, h, i, cb=cb: (0, cb * HEADS + h))
    vec = pl.BlockSpec((1, DH), lambda n, h, i: (0, 0))
    kv_out = pl.BlockSpec((None, N_META + t, DH), lambda n, h, i: (n, 0, h))
    bias_b = jnp.broadcast_to(bias.astype(F32).reshape(HEADS, 1, 1), (HEADS, 1, SB_BLOCK))
    out = pl.pallas_call(
        functools.partial(_sb_prompt_kernel, qtile=qtile, n_side=len(side)),
        grid=(nb, HEADS, nq),
        in_specs=[blk(COL_QB), blk(COL_KB), blk(COL_VB), mblk(COL_KB), mblk(COL_VB), vec, vec,
                  pl.BlockSpec((None, 1, SB_BLOCK), lambda n, h, i: (h, 0, 0))] + side_in_specs,
        out_specs=[pl.BlockSpec((qtile, DH), lambda n, h, i: (n * nq + i, h)), kv_out, kv_out] + side_out_specs,
        out_shape=[jax.ShapeDtypeStruct((nb * t, HW), BF),
                   jax.ShapeDtypeStruct((nb, N_META + t, HW), F32),
                   jax.ShapeDtypeStruct((nb, N_META + t, HW), F32)] + side_shapes,
        scratch_shapes=[pltpu.VMEM((t, DH), BF), pltpu.VMEM((t, DH), BF), pltpu.VMEM((t, DH), BF),
                        pltpu.VMEM((SB_BLOCK, DH), BF), pltpu.VMEM((SB_BLOCK, DH), BF),
                        pltpu.VMEM((2 * SB_BLOCK, 2 * SB_BLOCK), BF),
                        pltpu.VMEM((qtile, SB_BLOCK), F32), pltpu.VMEM((qtile, DH), F32)],
        compiler_params=_params(("arbitrary", "arbitrary", "arbitrary")),
        name="sb_prompt",
    )(proj, proj, proj, proj_meta, proj_meta, gq.reshape(1, DH), gk.reshape(1, DH), bias_b, *side)
    return out[0], out[1], out[2], out[3:]


def _sb_sample_kernel(pt_ref, qb_ref, kb_ref, vb_ref, *refs, tn, n_steps, pages_per_step):
    page_refs = refs[:4 * pages_per_step]
    gq_ref, gk_ref, bias_ref, ob_ref, ko_ref, vo_ref, q_s, c_s, acc_s, w_s = refs[4 * pages_per_step:]
    p = pl.program_id(1)
    rq = 16
    ro = SUBLANES
    half = HEADS // 2
    pg = page_refs[0].shape[0]
    scale = 1.0 / math.sqrt(DH)
    zrows = jnp.zeros((rq - ro, pg), F32)

    def sweep(blocks, mask):
        nblk = len(blocks)
        hr = HEADS * ro
        z = jnp.concatenate(
            [jnp.dot(q_s[h * rq:(h + 1) * rq, :], get_k(h).T.astype(BF), preferred_element_type=F32)[:ro]
             for get_k, _ in blocks for h in range(HEADS)], axis=0)
        z = (z.reshape(nblk, hr, pg) * scale + bias_ref[...]).reshape(nblk * hr, pg)
        c0 = c_s[...]
        logw, tot = _sb_tail(z, w_s[...], mask)
        cs = []
        for s in range(nblk):
            cs.append(c0)
            c0 = c0 + tot[s * hr:(s + 1) * hr]
        c_s[...] = c0
        a = _sb_exp(logw, jnp.concatenate(cs, axis=0), mask)
        acc = acc_s[...]
        for s, (_, get_v) in enumerate(blocks):
            acc = acc + jnp.concatenate(
                [jnp.dot(jnp.concatenate([a[s * hr + h * ro:s * hr + (h + 1) * ro], zrows], axis=0).astype(BF),
                         get_v(h).astype(BF), preferred_element_type=F32)[:ro] for h in range(HEADS)], axis=0)
        acc_s[...] = acc

    @pl.when(p == 0)
    def _():
        w_s[...] = _sb_w2(pg)
        c_s[...] = jnp.zeros(c_s.shape, F32)
        acc_s[...] = jnp.zeros(acc_s.shape, F32)
        q_s[...] = jnp.zeros(q_s.shape, BF)
        vo_ref[...] = vb_ref[...]
        kpad = jnp.zeros((pg - tn, DH), F32)
        ks, vs = [], []
        for h in range(HEADS):
            cols = slice(h * DH, (h + 1) * DH)
            q_s[h * rq:h * rq + tn, :] = _row_rmsnorm(qb_ref[:, cols], gq_ref[...]).astype(BF)
            kn = _row_rmsnorm(kb_ref[:, cols], gk_ref[...])
            ko_ref[:, cols] = kn
            ks.append(jnp.concatenate([kn, kpad], axis=0))
            vs.append(jnp.concatenate([vb_ref[:, cols], kpad], axis=0))
        row = lax.broadcasted_iota(jnp.int32, (HEADS * ro, pg), 0) % ro
        col = lax.broadcasted_iota(jnp.int32, (HEADS * ro, pg), 1)
        sweep([(lambda h: ks[h], lambda h: vs[h])], jnp.logical_and(col < row, col < tn))

    def head(lo, hi, h):
        return (lo if h < half else hi)[pl.ds(h % half, pg, stride=half), :]

    blocks = []
    for s in range(pages_per_step):
        k_lo, k_hi, v_lo, v_hi = [r.reshape(pg * half, DH) for r in page_refs[4 * s:4 * s + 4]]
        blocks.append((functools.partial(head, k_lo, k_hi), functools.partial(head, v_lo, v_hi)))
    sweep(blocks, None)

    @pl.when(p == n_steps - 1)
    def _():
        for h in range(HEADS):
            ob_ref[:, h * DH:(h + 1) * DH] = acc_s[h * ro:h * ro + tn, :].astype(ob_ref.dtype)


def _sb_sample(proj3, cache_k, cache_v, page_table, gq, gk, bias):
    db, tn, _ = proj3.shape
    n_pages = page_table.shape[1]
    pg = cache_k.shape[1]
    pps = SB_PAGES_PER_STEP if n_pages % SB_PAGES_PER_STEP == 0 else 1
    n_steps = n_pages // pps
    half = HEADS // 2
    assert tn <= SUBLANES and half == SUBLANES
    blk = lambda cb: pl.BlockSpec((None, tn, HW), lambda b, p, pt, cb=cb: (b, 0, cb))

    def page(s, g):
        return pl.BlockSpec((None, pg, half, DH),
                            lambda b, p, pt, s=s, g=g: (pt[b, n_pages - 1 - (p * pps + s)], 0, g, 0))

    pages, page_specs = [], []
    for s in range(pps):
        pages += [cache_k, cache_k, cache_v, cache_v]
        page_specs += [page(s, 0), page(s, 1), page(s, 0), page(s, 1)]
    vec = pl.BlockSpec((1, DH), lambda b, p, pt: (0, 0))
    out = pl.BlockSpec((None, tn, HW), lambda b, p, pt: (b, 0, 0))
    bias_b = jnp.broadcast_to(bias.astype(F32).reshape(HEADS, 1, 1), (HEADS, SUBLANES, pg)).reshape(HEADS * SUBLANES, pg)
    return pl.pallas_call(
        functools.partial(_sb_sample_kernel, tn=tn, n_steps=n_steps, pages_per_step=pps),
        grid_spec=pltpu.PrefetchScalarGridSpec(
            num_scalar_prefetch=1,
            grid=(db, n_steps),
            in_specs=[blk(COL_QB), blk(COL_KB), blk(COL_VB)] + page_specs + [
                vec, vec, pl.BlockSpec((HEADS * SUBLANES, pg), lambda b, p, pt: (0, 0))],
            out_specs=[out, out, out],
            scratch_shapes=[pltpu.VMEM((HEADS * 16, DH), BF), pltpu.VMEM((HEADS * SUBLANES, pg), F32),
                            pltpu.VMEM((HEADS * SUBLANES, DH), F32), pltpu.VMEM((2 * pg, 2 * pg), BF)]),
        out_shape=[jax.ShapeDtypeStruct((db, tn, HW), BF),
                   jax.ShapeDtypeStruct((db, tn, HW), F32),
                   jax.ShapeDtypeStruct((db, tn, HW), F32)],
        compiler_params=_params(("parallel", "arbitrary")),
        name="sb_sample",
    )(page_table, proj3, proj3, proj3, *pages, gq.reshape(1, DH), gk.reshape(1, DH), bias_b)


def _finish(x, oa, ob, proj, w_pa, w_pb, w_o, g_ffn, w_gate, w_up, w_down, d_ff, tm):
    d = x.shape[1]
    merged = _dense("gated_sum", [oa, ob], [w_pa, w_pb], d, [proj, proj], [COL_GTA, COL_GTA + d], BF, tm,
                    "merge_proj")
    x1 = _dense("residual", [merged], [w_o], d, [x], [0], F32, tm, "out_proj")
    h = _rmsnorm_bf16(x1, g_ffn)
    act = _dense("swiglu", [h], [w_gate, w_up], d_ff, [], [], BF, tm, "ffn_up")
    return _dense("residual", [act], [w_down], d, [x1], [0], F32, min(tm, 512), "ffn_down")


def kernel(x_prompt, x_sample, cache_k, cache_v, state_hgrn, page_table, meta_tokens, g_mix, w_in,
           hgrn_lb_logits, hgrn_g_norm, sb_gq, sb_gk, sb_bias, w_pa, w_pb, w_o, g_ffn, w_gate, w_up, w_down):
    nb, t, d = x_prompt.shape
    db, tn, _ = x_sample.shape
    depth = w_in.shape[0]
    assert depth == 1, "single-layer trunk"
    assert t % GLA_ROWS == 0 and N_META % GLA_CHUNK == 0
    l = 0
    lb = jnp.cumsum(jax.nn.softmax(hgrn_lb_logits.astype(F32), axis=0), axis=0)[l]
    lead = lambda a: a.reshape(a.shape[1:])
    in_width = w_in.shape[-1]
    w_in_b = lead(w_in).astype(BF)

    xp = x_prompt.reshape(nb * t, d)
    xs = x_sample.reshape(db * tn, d)

    def in_proj(x, tm):
        return _dense("plain", [_rmsnorm_bf16(x, g_mix[l])], [w_in_b], in_width, [], [], F32, tm, "in_proj")

    proj_m = in_proj(meta_tokens.astype(x_prompt.dtype), N_META)
    proj_p = in_proj(xp, 1024)
    proj_s = in_proj(xs, 512)
    proj_s3 = proj_s.reshape(db, tn, -1)

    oa_p, s_p = _gla_prompt(proj_p, proj_m, lb, hgrn_g_norm[l], nb, t)
    ob_p, k_p, v_p, (w_pa_b, w_pb_b, w_o_b, w_gate_b, w_up_b, w_down_b) = _sb_prompt(
        proj_p, proj_m, sb_gq[l], sb_gk[l], sb_bias[l], nb, t,
        [lead(w) for w in (w_pa, w_pb, w_o, w_gate, w_up, w_down)])
    lw = (w_pa_b, w_pb_b, w_o_b, g_ffn[l], w_gate_b, w_up_b, w_down_b)
    d_ff = w_gate.shape[-1]
    y_p = _finish(xp, oa_p, ob_p, proj_p, *lw, d_ff, tm=1024)

    oa_s, s_s = _gla_sample(proj_s3, lead(state_hgrn), lb, hgrn_g_norm[l])
    ob_s, k_s, v_s = _sb_sample(proj_s3, lead(cache_k), lead(cache_v), page_table, sb_gq[l], sb_gk[l],
                                sb_bias[l])
    y_s = _finish(xs, oa_s.reshape(db * tn, HW), ob_s.reshape(db * tn, HW), proj_s, *lw, d_ff, tm=512)

    return (y_p.reshape(nb, t, d), y_s.reshape(db, tn, d),
            k_p.reshape(1, nb, N_META + t, HEADS, DH), v_p.reshape(1, nb, N_META + t, HEADS, DH),
            s_p[None],
            k_s.reshape(1, db, tn, HEADS, DH).astype(cache_k.dtype),
            v_s.reshape(1, db, tn, HEADS, DH).astype(cache_v.dtype),
            s_s[None].astype(state_hgrn.dtype))
```

```python
import functools
import math

import jax
import jax.numpy as jnp
from jax import lax
from jax.experimental import pallas as pl
from jax.experimental.pallas import tpu as pltpu

N_META = 16
HEADS = 16
DH = 128
HW = HEADS * DH
EPS = 1e-6
SUBLANES = 8
BF16_ROWS = 16
GLA_CHUNK = 16
GLA_ROWS = 128
GLA_UNROLL = 16
SB_BLOCK = 128
SB_QTILE = 2048
SB_PAGES_PER_STEP = 4
EXP_CLAMP = 80.0
VMEM_LIMIT = 56 * 1024 * 1024
IN_TN = 1024
TAIL_TN = 512

COL_QA, COL_FA, COL_IA, COL_OG, COL_QB, COL_KB, COL_VB = range(7)
COL_GTA = 7 * HW

_NT = (((1,), (1,)), ((), ()))
_TN = (((0,), (0,)), ((), ()))
BF = jnp.bfloat16
F32 = jnp.float32


def _params(sem):
    return pltpu.CompilerParams(dimension_semantics=sem, vmem_limit_bytes=VMEM_LIMIT)


def _sigmoid(x):
    return 1.0 / (1.0 + jnp.exp(-x))


def _row_rmsnorm(x, g):
    return x * lax.rsqrt(jnp.mean(x * x, axis=-1, keepdims=True) + EPS) * g


def _rmsnorm_kernel(x_ref, g_ref, o_ref):
    o_ref[...] = _row_rmsnorm(x_ref[...], g_ref[...]).astype(o_ref.dtype)


def _rmsnorm_bf16(x, g, tm=256):
    m, d = x.shape
    tm = min(tm, m)
    return pl.pallas_call(
        _rmsnorm_kernel,
        grid=(pl.cdiv(m, tm),),
        in_specs=[pl.BlockSpec((tm, d), lambda i: (i, 0)), pl.BlockSpec((1, d), lambda i: (0, 0))],
        out_specs=pl.BlockSpec((tm, d), lambda i: (i, 0)),
        out_shape=jax.ShapeDtypeStruct((m, d), BF),
        compiler_params=_params(("parallel",)),
        name="rmsnorm_bf16",
    )(x, g.reshape(1, d))


def _dense_kernel(*refs, mode):
    o_ref = refs[-1]
    if mode == "plain":
        a, w = refs[:2]
        o_ref[...] = jnp.dot(a[...], w[...], preferred_element_type=F32).astype(o_ref.dtype)
    elif mode == "residual":
        a, w, r = refs[:3]
        o_ref[...] = r[...] + jnp.dot(a[...], w[...], preferred_element_type=F32)
    elif mode == "swiglu":
        a, wg, wu = refs[:3]
        g = jnp.dot(a[...], wg[...], preferred_element_type=F32)
        u = jnp.dot(a[...], wu[...], preferred_element_type=F32)
        o_ref[...] = (g * _sigmoid(g) * u).astype(o_ref.dtype)
    elif mode == "gated_sum":
        a, b, wa, wb, ga, gb = refs[:6]
        pa = jnp.dot(a[...], wa[...], preferred_element_type=F32)
        pb = jnp.dot(b[...], wb[...], preferred_element_type=F32)
        o_ref[...] = (_sigmoid(ga[...]) * pa + _sigmoid(gb[...]) * pb).astype(o_ref.dtype)
    else:
        raise ValueError(mode)


def _dense(mode, acts, weights, n, extras, extra_col_offsets, out_dtype, tm, name):
    m = acts[0].shape[0]
    tm = min(tm, m)
    in_specs = [pl.BlockSpec((tm, a.shape[1]), lambda i, j: (i, 0)) for a in acts]
    if weights[0].ndim == 3:
        tn = weights[0].shape[2]
        assert all(w.shape[0] == pl.cdiv(n, tn) and w.shape[2] == tn for w in weights)
        in_specs += [pl.BlockSpec((None, w.shape[1], tn), lambda i, j: (j, 0, 0)) for w in weights]
    else:
        tn = min(IN_TN, n)
        assert all(w.shape[1] == n for w in weights)
        in_specs += [pl.BlockSpec((w.shape[0], tn), lambda i, j: (0, j)) for w in weights]
    for off in extra_col_offsets:
        assert off % tn == 0
        in_specs.append(pl.BlockSpec((tm, tn), lambda i, j, o=off // tn: (i, j + o)))
    return pl.pallas_call(
        functools.partial(_dense_kernel, mode=mode),
        grid=(pl.cdiv(m, tm), pl.cdiv(n, tn)),
        in_specs=in_specs,
        out_specs=pl.BlockSpec((tm, tn), lambda i, j: (i, j)),
        out_shape=jax.ShapeDtypeStruct((m, n), out_dtype),
        compiler_params=_params(("parallel", "arbitrary")),
        name=name,
    )(*acts, *weights, *extras)


def _side_cast_specs(side, grid, tn):
    steps = math.prod(grid)
    strides = [math.prod(grid[a + 1:]) for a in range(len(grid))]
    in_specs, out_specs, shapes = [], [], []
    for x in side:
        rows, cols = x.shape
        r = BF16_ROWS * pl.cdiv(rows, BF16_ROWS * steps)
        while rows % r and r < rows:
            r += BF16_ROWS
        last = pl.cdiv(rows, r) - 1
        tw = min(tn, cols)
        nblk = pl.cdiv(cols, tw)

        def chunk(*idx, last=last):
            return jnp.minimum(sum(i * st for i, st in zip(idx[:len(grid)], strides)), last)

        in_specs.append(pl.BlockSpec((r, cols), lambda *idx, chunk=chunk: (chunk(*idx), 0)))
        out_specs.append(pl.BlockSpec((nblk, r, tw), lambda *idx, chunk=chunk: (0, chunk(*idx), 0)))
        shapes.append(jax.ShapeDtypeStruct((nblk, rows, tw), BF))
    return in_specs, out_specs, shapes


def _side_cast(side_in, side_out):
    for src, dst in zip(side_in, side_out):
        nblk, r, tw = dst.shape
        cols = src.shape[1]
        for j in range(nblk):
            w = min(tw, cols - j * tw)
            dst[j, :, :w] = src[:, j * tw:j * tw + w].astype(dst.dtype)
            if w < tw:
                dst[j, :, w:] = jnp.zeros((r, tw - w), dst.dtype)


def _split3(x):
    hi = x.astype(BF)
    r = x - hi.astype(F32)
    mid = r.astype(BF)
    lo = (r - mid.astype(F32)).astype(BF)
    return hi, mid, lo


def _gla_masks(r, c):
    row = lax.broadcasted_iota(jnp.int32, (r, r), 0)
    col = lax.broadcasted_iota(jnp.int32, (r, r), 1)
    rid = lax.broadcasted_iota(jnp.int32, (r, 1), 0)
    levels = []
    size = r
    while size > c:
        half = size // 2
        mask = jnp.logical_and(row // size == col // size,
                               jnp.logical_and(row % size >= half, col % size < half))
        levels.append((size, rid % size >= half, mask))
        size = half
    return row >= col, levels, jnp.logical_and(row >= col, row // c == col // c)


def _gla_front(qa, fr, ia, lb, masks, c, valid=None):
    r = qa.shape[0]
    tri, levels, chunk_mask = masks
    e = jnp.exp(-jnp.abs(fr))
    inv = 1.0 / (1.0 + e)
    pos = fr >= 0
    sig = jnp.where(pos, inv, e * inv)
    nsig = jnp.where(pos, e * inv, inv)
    logf = jnp.log(lb + (1.0 - lb) * sig)
    k = (1.0 - lb) * nsig
    if valid is not None:
        logf = jnp.where(valid, logf, 0.0)
        k = jnp.where(valid, k, 0.0)
    q = qa * _sigmoid(qa)
    v = ia.astype(BF)
    tri_b = tri.astype(BF)
    hi, mid, lo = _split3(logf)
    b = (jnp.dot(tri_b, hi, preferred_element_type=F32) + jnp.dot(tri_b, mid, preferred_element_type=F32)
         + jnp.dot(tri_b, lo, preferred_element_type=F32))

    def at_row(size, i):
        b3 = b.reshape(r // size, size, DH)
        return jnp.broadcast_to(b3[:, i:i + 1, :], b3.shape).reshape(r, DH)

    bm = at_row(c, c // 2 - 1)
    qh = (q * jnp.exp(jnp.minimum(b - bm, EXP_CLAMP))).astype(BF)
    kh = (k * jnp.exp(jnp.minimum(bm - b, EXP_CLAMP))).astype(BF)
    attn = jnp.where(chunk_mask, lax.dot_general(qh, kh, _NT, preferred_element_type=F32), 0.0)
    for size, upper, mask in levels:
        bs = at_row(size, size // 2 - 1)
        x = (jnp.where(upper, q, k) * jnp.exp(jnp.where(upper, b - bs, bs - b))).astype(BF)
        attn = attn + jnp.where(mask, lax.dot_general(x, x, _NT, preferred_element_type=F32), 0.0)
    return q, k, v, b, jnp.dot(attn.astype(BF), v, preferred_element_type=F32), at_row


def _gla_rows(st, qa, fr, ia, lb, masks, c):
    r = qa.shape[0]
    q, k, v, b, o, _ = _gla_front(qa, fr, ia, lb, masks, c)
    qs = (q * jnp.exp(b)).astype(BF)
    bl = b[r - 1:r, :]
    kt = (k * jnp.exp(bl - b)).astype(BF)
    o = o + lax.dot_general(qs, st.astype(BF), _NT, preferred_element_type=F32)
    st = st * jnp.exp(bl) + lax.dot_general(v, kt, _TN, preferred_element_type=F32)
    return st, o


def _gla_out(o, og, gn):
    return (_row_rmsnorm(o, gn) * (og * _sigmoid(og))).astype(BF)


def _gla_prompt_kernel(qa_ref, fa_ref, ia_ref, og_ref, mfa_ref, mia_ref, lb_ref, gn_ref,
                       oa_ref, s_ref, *, n_iters):
    lb = lb_ref[...]
    gn = gn_ref[...]
    st, _ = _gla_rows(jnp.zeros((DH, DH), F32), jnp.zeros((N_META, DH), F32), mfa_ref[...], mia_ref[...], lb,
                      _gla_masks(N_META, GLA_CHUNK), GLA_CHUNK)
    masks = _gla_masks(GLA_ROWS, GLA_CHUNK)

    def body(i, st):
        rows = pl.ds(pl.multiple_of(i * GLA_ROWS, GLA_ROWS), GLA_ROWS)
        st, o = _gla_rows(st, qa_ref[rows, :], fa_ref[rows, :], ia_ref[rows, :], lb, masks, GLA_CHUNK)
        oa_ref[rows, :] = _gla_out(o, og_ref[rows, :], gn)
        return st

    st = lax.fori_loop(0, n_iters, body, st, unroll=GLA_UNROLL if n_iters % GLA_UNROLL == 0 else 1)
    s_ref[...] = st.T


def _gla_prompt(proj, proj_meta, meta_blk, lb, g_norm, nb, t):
    blk = lambda cb: pl.BlockSpec((t, DH), lambda n, h, cb=cb: (n, cb * HEADS + h))
    mblk = lambda cb: pl.BlockSpec((N_META, DH), lambda n, h, cb=cb: (meta_blk, cb * HEADS + h))
    return pl.pallas_call(
        functools.partial(_gla_prompt_kernel, n_iters=t // GLA_ROWS),
        grid=(nb, HEADS),
        in_specs=[blk(COL_QA), blk(COL_FA), blk(COL_IA), blk(COL_OG), mblk(COL_FA), mblk(COL_IA),
                  pl.BlockSpec((1, DH), lambda n, h: (0, h)), pl.BlockSpec((1, DH), lambda n, h: (0, 0))],
        out_specs=[pl.BlockSpec((t, DH), lambda n, h: (n, h)),
                   pl.BlockSpec((None, None, DH, DH), lambda n, h: (n, h, 0, 0))],
        out_shape=[jax.ShapeDtypeStruct((nb * t, HW), BF),
                   jax.ShapeDtypeStruct((nb, HEADS, DH, DH), F32)],
        compiler_params=_params(("parallel", "parallel")),
        name="gla_prompt",
    )(proj, proj, proj, proj, proj_meta, proj_meta, lb.reshape(1, HW), g_norm.reshape(1, DH))


def _gla_sample_kernel(qa_ref, fa_ref, ia_ref, og_ref, s0_ref, lb_ref, gn_ref, oa_ref, s_ref, *, tn):
    c = SUBLANES
    r = HEADS * c
    row = lax.broadcasted_iota(jnp.int32, (r, r), 0)
    col = lax.broadcasted_iota(jnp.int32, (r, r), 1)
    own = jnp.logical_and(row >= col, row // c == col // c)
    valid = lax.broadcasted_iota(jnp.int32, (r, DH), 0) % c < tn
    pad = jnp.zeros((c - tn, DH), F32)

    def stack(ref):
        return jnp.concatenate([x for h in range(HEADS) for x in (ref[:, h * DH:(h + 1) * DH], pad)], axis=0)

    lb = jnp.concatenate([jnp.broadcast_to(lb_ref[:, h * DH:(h + 1) * DH], (c, DH)) for h in range(HEADS)], axis=0)
    q, k, v, b, o, at_row = _gla_front(stack(qa_ref), stack(fa_ref), stack(ia_ref), lb, (own, [], own), c, valid)
    qs = (q * jnp.exp(b)).astype(BF)
    bl = at_row(c, c - 1)
    kt = (k * jnp.exp(bl - b)).astype(BF)
    decay = jnp.exp(bl)
    o_state = []
    for h in range(HEADS):
        rows = slice(h * c, (h + 1) * c)
        st = s0_ref[h]
        o_state.append(jnp.dot(qs[rows], st.astype(BF), preferred_element_type=F32))
        decay_col = decay[rows].T[:, 0:1]
        s_ref[h] = st * decay_col + lax.dot_general(kt[rows], v[rows], _TN, preferred_element_type=F32)
    out = _gla_out(o + jnp.concatenate(o_state, axis=0), stack(og_ref), gn_ref[...])
    for h in range(HEADS):
        oa_ref[:, h * DH:(h + 1) * DH] = out[h * c:h * c + tn, :]


def _gla_sample(proj3, state, lb, g_norm):
    db, tn, _ = proj3.shape
    assert tn <= SUBLANES
    blk = lambda cb: pl.BlockSpec((None, tn, HW), lambda b, cb=cb: (b, 0, cb))
    sblk = pl.BlockSpec((None, HEADS, DH, DH), lambda b: (b, 0, 0, 0))
    return pl.pallas_call(
        functools.partial(_gla_sample_kernel, tn=tn),
        grid=(db,),
        in_specs=[blk(COL_QA), blk(COL_FA), blk(COL_IA), blk(COL_OG), sblk,
                  pl.BlockSpec((1, HW), lambda b: (0, 0)), pl.BlockSpec((1, DH), lambda b: (0, 0))],
        out_specs=[pl.BlockSpec((None, tn, HW), lambda b: (b, 0, 0)), sblk],
        out_shape=[jax.ShapeDtypeStruct((db, tn, HW), BF),
                   jax.ShapeDtypeStruct((db, HEADS, DH, DH), F32)],
        compiler_params=_params(("parallel",)),
        name="gla_sample",
    )(proj3, proj3, proj3, proj3, state, lb.reshape(1, HW), g_norm.reshape(1, DH))


def _sb_tail(z, w2, mask):
    kk = z.shape[1]
    log_beta = jnp.minimum(z, 0.0) - jnp.log(1.0 + jnp.exp(-jnp.abs(z)))
    l1m = log_beta - z
    if mask is not None:
        l1m = jnp.where(mask, l1m, 0.0)
    hi = l1m.astype(BF)
    lo = (l1m - hi.astype(F32)).astype(BF)
    r = jnp.dot(jnp.concatenate([hi, lo], axis=1), w2, preferred_element_type=F32)
    return log_beta + r[:, :kk], r[:, kk:]


def _sb_exp(logw, c, mask):
    a = jnp.exp(logw + c)
    return a if mask is None else jnp.where(mask, a, 0.0)


def _sb_w2(kk):
    row = lax.broadcasted_iota(jnp.int32, (2 * kk, 2 * kk), 0) % kk
    col = lax.broadcasted_iota(jnp.int32, (2 * kk, 2 * kk), 1)
    return jnp.logical_or(col >= kk, row > col).astype(BF)


def _sb_prompt_kernel(qb_ref, kb_ref, vb_ref, mk_ref, mv_ref, gq_ref, gk_ref, bias_ref, *refs, qtile, n_side):
    side_in, (ob_ref, ko_ref, vo_ref), side_out = refs[:n_side], refs[n_side:n_side + 3], refs[n_side + 3:2 * n_side + 3]
    q_s, k_s, v_s, mk_s, mv_s, w_s, c_s, acc_s = refs[2 * n_side + 3:]
    _side_cast(side_in, side_out)
    it = pl.program_id(2)
    blk = SB_BLOCK
    nsub = qtile // blk
    scale = 1.0 / math.sqrt(DH)

    @pl.when(it == 0)
    def _():
        kn = _row_rmsnorm(kb_ref[...], gk_ref[...])
        ko_ref[N_META:, :] = kn
        k_s[...] = kn.astype(BF)
        mkn = _row_rmsnorm(mk_ref[...], gk_ref[...])
        ko_ref[:N_META, :] = mkn
        mk_s[...] = jnp.zeros(mk_s.shape, BF)
        mk_s[:N_META, :] = mkn.astype(BF)
        v = vb_ref[...]
        vo_ref[N_META:, :] = v
        v_s[...] = v.astype(BF)
        mv = mv_ref[...]
        vo_ref[:N_META, :] = mv
        mv_s[...] = jnp.zeros(mv_s.shape, BF)
        mv_s[:N_META, :] = mv.astype(BF)
        q_s[...] = _row_rmsnorm(qb_ref[...], gq_ref[...]).astype(BF)
        w_s[...] = _sb_w2(blk)

    bias = bias_ref[...]
    w2 = w_s[...]
    c_s[...] = jnp.zeros(c_s.shape, F32)
    acc_s[...] = jnp.zeros(acc_s.shape, F32)

    def sweep(r0, kj, vj, visible):
        nr = qtile - r0
        rows = pl.ds(r0, nr)
        qi = q_s[pl.ds(pl.multiple_of(it * qtile, qtile) + r0, nr), :]
        z = lax.dot_general(qi, kj, _NT, preferred_element_type=F32) * scale + bias
        col = lax.broadcasted_iota(jnp.int32, (nr, blk), 1)
        if visible is None:
            m = None
        elif visible == "causal":
            m = col < lax.broadcasted_iota(jnp.int32, (nr, blk), 0)
        else:
            m = col < visible
        logw, tot = _sb_tail(z, w2, m)
        c = c_s[rows, :]
        a = _sb_exp(logw, c, m)
        c_s[rows, :] = c + tot
        acc_s[rows, :] += jnp.dot(a.astype(BF), vj, preferred_element_type=F32)

    def key_rows(j):
        return pl.ds(pl.multiple_of(j * blk, blk), blk)

    for s in reversed(range(nsub)):
        j = it * nsub + s
        sweep(s * blk, k_s[key_rows(j), :], v_s[key_rows(j), :], "causal")

    def body(jj, carry):
        for u in range(nsub):
            j = (it - jj) * nsub - 1 - u
            sweep(0, k_s[key_rows(j), :], v_s[key_rows(j), :], None)
        return carry

    lax.fori_loop(0, it, body, 0)
    sweep(0, mk_s[...], mv_s[...], N_META)
    ob_ref[...] = acc_s[...].astype(ob_ref.dtype)


def _sb_prompt(proj, proj_meta, meta_blk, gq, gk, bias, nb, t, side):
    qtile = min(SB_QTILE, t)
    assert t % qtile == 0 and qtile % SB_BLOCK == 0
    nq = t // qtile
    side_in_specs, side_out_specs, side_shapes = _side_cast_specs(side, (nb, HEADS, nq), TAIL_TN)
    blk = lambda cb: pl.BlockSpec((t, DH), lambda n, h, i, cb=cb: (n, cb * HEADS + h))
    mblk = lambda cb: pl.BlockSpec((N_META, DH), lambda n, h, i, cb=cb: (meta_blk, cb * HEADS + h))
    vec = pl.BlockSpec((1, DH), lambda n, h, i: (0, 0))
    kv_out = pl.BlockSpec((None, N_META + t, DH), lambda n, h, i: (n, 0, h))
    bias_b = jnp.broadcast_to(bias.astype(F32).reshape(HEADS, 1, 1), (HEADS, 1, SB_BLOCK))
    out = pl.pallas_call(
        functools.partial(_sb_prompt_kernel, qtile=qtile, n_side=len(side)),
        grid=(nb, HEADS, nq),
        in_specs=[blk(COL_QB), blk(COL_KB), blk(COL_VB), mblk(COL_KB), mblk(COL_VB), vec, vec,
                  pl.BlockSpec((None, 1, SB_BLOCK), lambda n, h, i: (h, 0, 0))] + side_in_specs,
        out_specs=[pl.BlockSpec((qtile, DH), lambda n, h, i: (n * nq + i, h)), kv_out, kv_out] + side_out_specs,
        out_shape=[jax.ShapeDtypeStruct((nb * t, HW), BF),
                   jax.ShapeDtypeStruct((nb, N_META + t, HW), F32),
                   jax.ShapeDtypeStruct((nb, N_META + t, HW), F32)] + side_shapes,
        scratch_shapes=[pltpu.VMEM((t, DH), BF), pltpu.VMEM((t, DH), BF), pltpu.VMEM((t, DH), BF),
                        pltpu.VMEM((SB_BLOCK, DH), BF), pltpu.VMEM((SB_BLOCK, DH), BF),
                        pltpu.VMEM((2 * SB_BLOCK, 2 * SB_BLOCK), BF),
                        pltpu.VMEM((qtile, SB_BLOCK), F32), pltpu.VMEM((qtile, DH), F32)],
        compiler_params=_params(("arbitrary", "arbitrary", "arbitrary")),
        name="sb_prompt",
    )(proj, proj, proj, proj_meta, proj_meta, gq.reshape(1, DH), gk.reshape(1, DH), bias_b, *side)
    return out[0], out[1], out[2], out[3:]


def _sb_sample_kernel(pt_ref, qb_ref, kb_ref, vb_ref, *refs, tn, n_steps, pages_per_step):
    page_refs = refs[:4 * pages_per_step]
    gq_ref, gk_ref, bias_ref, ob_ref, ko_ref, vo_ref, q_s, c_s, acc_s, w_s = refs[4 * pages_per_step:]
    p = pl.program_id(1)
    rq = 16
    ro = SUBLANES
    half = HEADS // 2
    pg = page_refs[0].shape[0]
    scale = 1.0 / math.sqrt(DH)
    zrows = jnp.zeros((rq - ro, pg), F32)

    def sweep(blocks, mask):
        nblk = len(blocks)
        hr = HEADS * ro
        z = jnp.concatenate(
            [jnp.dot(q_s[h * rq:(h + 1) * rq, :], get_k(h).T.astype(BF), preferred_element_type=F32)[:ro]
             for get_k, _ in blocks for h in range(HEADS)], axis=0)
        z = (z.reshape(nblk, hr, pg) * scale + bias_ref[...]).reshape(nblk * hr, pg)
        c0 = c_s[...]
        logw, tot = _sb_tail(z, w_s[...], mask)
        cs = []
        for s in range(nblk):
            cs.append(c0)
            c0 = c0 + tot[s * hr:(s + 1) * hr]
        c_s[...] = c0
        a = _sb_exp(logw, jnp.concatenate(cs, axis=0), mask)
        acc = acc_s[...]
        for s, (_, get_v) in enumerate(blocks):
            acc = acc + jnp.concatenate(
                [jnp.dot(jnp.concatenate([a[s * hr + h * ro:s * hr + (h + 1) * ro], zrows], axis=0).astype(BF),
                         get_v(h).astype(BF), preferred_element_type=F32)[:ro] for h in range(HEADS)], axis=0)
        acc_s[...] = acc

    @pl.when(p == 0)
    def _():
        w_s[...] = _sb_w2(pg)
        c_s[...] = jnp.zeros(c_s.shape, F32)
        acc_s[...] = jnp.zeros(acc_s.shape, F32)
        q_s[...] = jnp.zeros(q_s.shape, BF)
        vo_ref[...] = vb_ref[...]
        kpad = jnp.zeros((pg - tn, DH), F32)
        ks, vs = [], []
        for h in range(HEADS):
            cols = slice(h * DH, (h + 1) * DH)
            q_s[h * rq:h * rq + tn, :] = _row_rmsnorm(qb_ref[:, cols], gq_ref[...]).astype(BF)
            kn = _row_rmsnorm(kb_ref[:, cols], gk_ref[...])
            ko_ref[:, cols] = kn
            ks.append(jnp.concatenate([kn, kpad], axis=0))
            vs.append(jnp.concatenate([vb_ref[:, cols], kpad], axis=0))
        row = lax.broadcasted_iota(jnp.int32, (HEADS * ro, pg), 0) % ro
        col = lax.broadcasted_iota(jnp.int32, (HEADS * ro, pg), 1)
        sweep([(lambda h: ks[h], lambda h: vs[h])], jnp.logical_and(col < row, col < tn))

    def head(lo, hi, h):
        return (lo if h < half else hi)[pl.ds(h % half, pg, stride=half), :]

    blocks = []
    for s in range(pages_per_step):
        k_lo, k_hi, v_lo, v_hi = [r.reshape(pg * half, DH) for r in page_refs[4 * s:4 * s + 4]]
        blocks.append((functools.partial(head, k_lo, k_hi), functools.partial(head, v_lo, v_hi)))
    sweep(blocks, None)

    @pl.when(p == n_steps - 1)
    def _():
        for h in range(HEADS):
            ob_ref[:, h * DH:(h + 1) * DH] = acc_s[h * ro:h * ro + tn, :].astype(ob_ref.dtype)


def _sb_sample(proj3, cache_k, cache_v, page_table, gq, gk, bias):
    db, tn, _ = proj3.shape
    n_pages = page_table.shape[1]
    pg = cache_k.shape[1]
    pps = SB_PAGES_PER_STEP if n_pages % SB_PAGES_PER_STEP == 0 else 1
    n_steps = n_pages // pps
    half = HEADS // 2
    assert tn <= SUBLANES and half == SUBLANES
    blk = lambda cb: pl.BlockSpec((None, tn, HW), lambda b, p, pt, cb=cb: (b, 0, cb))

    def page(s, g):
        return pl.BlockSpec((None, pg, half, DH),
                            lambda b, p, pt, s=s, g=g: (pt[b, n_pages - 1 - (p * pps + s)], 0, g, 0))

    pages, page_specs = [], []
    for s in range(pps):
        pages += [cache_k, cache_k, cache_v, cache_v]
        page_specs += [page(s, 0), page(s, 1), page(s, 0), page(s, 1)]
    vec = pl.BlockSpec((1, DH), lambda b, p, pt: (0, 0))
    out = pl.BlockSpec((None, tn, HW), lambda b, p, pt: (b, 0, 0))
    bias_b = jnp.broadcast_to(bias.astype(F32).reshape(HEADS, 1, 1), (HEADS, SUBLANES, pg)).reshape(HEADS * SUBLANES, pg)
    return pl.pallas_call(
        functools.partial(_sb_sample_kernel, tn=tn, n_steps=n_steps, pages_per_step=pps),
        grid_spec=pltpu.PrefetchScalarGridSpec(
            num_scalar_prefetch=1,
            grid=(db, n_steps),
            in_specs=[blk(COL_QB), blk(COL_KB), blk(COL_VB)] + page_specs + [
                vec, vec, pl.BlockSpec((HEADS * SUBLANES, pg), lambda b, p, pt: (0, 0))],
            out_specs=[out, out, out],
            scratch_shapes=[pltpu.VMEM((HEADS * 16, DH), BF), pltpu.VMEM((HEADS * SUBLANES, pg), F32),
                            pltpu.VMEM((HEADS * SUBLANES, DH), F32), pltpu.VMEM((2 * pg, 2 * pg), BF)]),
        out_shape=[jax.ShapeDtypeStruct((db, tn, HW), BF),
                   jax.ShapeDtypeStruct((db, tn, HW), F32),
                   jax.ShapeDtypeStruct((db, tn, HW), F32)],
        compiler_params=_params(("parallel", "arbitrary")),
        name="sb_sample",
    )(page_table, proj3, proj3, proj3, *pages, gq.reshape(1, DH), gk.reshape(1, DH), bias_b)


def _finish(x, oa, ob, proj, w_pa, w_pb, w_o, g_ffn, w_gate, w_up, w_down, d_ff, tm):
    d = x.shape[1]
    merged = _dense("gated_sum", [oa, ob], [w_pa, w_pb], d, [proj, proj], [COL_GTA, COL_GTA + d], BF, tm,
                    "merge_proj")
    x1 = _dense("residual", [merged], [w_o], d, [x], [0], F32, tm, "out_proj")
    h = _rmsnorm_bf16(x1, g_ffn)
    act = _dense("swiglu", [h], [w_gate, w_up], d_ff, [], [], BF, tm, "ffn_up")
    return _dense("residual", [act], [w_down], d, [x1], [0], F32, min(tm, 512), "ffn_down")


def kernel(x_prompt, x_sample, cache_k, cache_v, state_hgrn, page_table, meta_tokens, g_mix, w_in,
           hgrn_lb_logits, hgrn_g_norm, sb_gq, sb_gk, sb_bias, w_pa, w_pb, w_o, g_ffn, w_gate, w_up, w_down):
    nb, t, d = x_prompt.shape
    db, tn, _ = x_sample.shape
    depth = w_in.shape[0]
    assert depth == 1, "single-layer trunk"
    assert t % GLA_ROWS == 0 and N_META % GLA_CHUNK == 0
    l = 0
    lb = jnp.cumsum(jax.nn.softmax(hgrn_lb_logits.astype(F32), axis=0), axis=0)[l]
    lead = lambda a: a.reshape(a.shape[1:])
    in_width = w_in.shape[-1]
    w_in_b = lead(w_in).astype(BF)

    xp = x_prompt.reshape(nb * t, d)
    xs = x_sample.reshape(db * tn, d)

    def in_proj(x, tm):
        return _dense("plain", [_rmsnorm_bf16(x, g_mix[l])], [w_in_b], in_width, [], [], F32, tm, "in_proj")

    ns = db * tn
    assert ns % N_META == 0, "decode rows must fill whole meta-sized row blocks"
    meta_blk = ns // N_META
    proj_s = in_proj(jnp.concatenate([xs, meta_tokens.astype(xs.dtype)], axis=0), ns + N_META)
    proj_p = in_proj(xp, 1024)
    proj_s3 = proj_s[:ns].reshape(db, tn, -1)

    oa_p, s_p = _gla_prompt(proj_p, proj_s, meta_blk, lb, hgrn_g_norm[l], nb, t)
    ob_p, k_p, v_p, (w_pa_b, w_pb_b, w_o_b, w_gate_b, w_up_b, w_down_b) = _sb_prompt(
        proj_p, proj_s, meta_blk, sb_gq[l], sb_gk[l], sb_bias[l], nb, t,
        [lead(w) for w in (w_pa, w_pb, w_o, w_gate, w_up, w_down)])
    lw = (w_pa_b, w_pb_b, w_o_b, g_ffn[l], w_gate_b, w_up_b, w_down_b)
    d_ff = w_gate.shape[-1]
    y_p = _finish(xp, oa_p, ob_p, proj_p, *lw, d_ff, tm=1024)

    oa_s, s_s = _gla_sample(proj_s3, lead(state_hgrn), lb, hgrn_g_norm[l])
    ob_s, k_s, v_s = _sb_sample(proj_s3, lead(cache_k), lead(cache_v), page_table, sb_gq[l], sb_gk[l],
                                sb_bias[l])
    y_s = _finish(xs, oa_s.reshape(db * tn, HW), ob_s.reshape(db * tn, HW), proj_s, *lw, d_ff, tm=512)

    return (y_p.reshape(nb, t, d), y_s.reshape(db, tn, d),
            k_p.reshape(1, nb, N_META + t, HEADS, DH), v_p.reshape(1, nb, N_META + t, HEADS, DH),
            s_p[None],
            k_s.reshape(1, db, tn, HEADS, DH).astype(cache_k.dtype),
            v_s.reshape(1, db, tn, HEADS, DH).astype(cache_v.dtype),
            s_s[None].astype(state_hgrn.dtype))
```

```python
import functools
import math

import jax
import jax.numpy as jnp
from jax import lax
from jax.experimental import pallas as pl
from jax.experimental.pallas import tpu as pltpu

N_META = 16
HEADS = 16
DH = 128
HW = HEADS * DH
EPS = 1e-6
SUBLANES = 8
BF16_ROWS = 16
GLA_CHUNK = 16
GLA_ROWS = 128
GLA_UNROLL = 16
SB_BLOCK = 128
SB_QTILE = 2048
SB_PAGES_PER_STEP = 4
EXP_CLAMP = 80.0
VMEM_LIMIT = 56 * 1024 * 1024
IN_TN = 1024
TAIL_TN = 512

COL_QA, COL_FA, COL_IA, COL_OG, COL_QB, COL_KB, COL_VB = range(7)
COL_GTA = 7 * HW

_NT = (((1,), (1,)), ((), ()))
_TN = (((0,), (0,)), ((), ()))
BF = jnp.bfloat16
F32 = jnp.float32


def _params(sem):
    return pltpu.CompilerParams(dimension_semantics=sem, vmem_limit_bytes=VMEM_LIMIT)


def _sigmoid(x):
    return 1.0 / (1.0 + jnp.exp(-x))


def _row_rmsnorm(x, g):
    return x * lax.rsqrt(jnp.mean(x * x, axis=-1, keepdims=True) + EPS) * g


def _rmsnorm_kernel(x_ref, g_ref, o_ref):
    o_ref[...] = _row_rmsnorm(x_ref[...], g_ref[...]).astype(o_ref.dtype)


def _rmsnorm_bf16(x, g, tm=256):
    m, d = x.shape
    tm = min(tm, m)
    return pl.pallas_call(
        _rmsnorm_kernel,
        grid=(pl.cdiv(m, tm),),
        in_specs=[pl.BlockSpec((tm, d), lambda i: (i, 0)), pl.BlockSpec((1, d), lambda i: (0, 0))],
        out_specs=pl.BlockSpec((tm, d), lambda i: (i, 0)),
        out_shape=jax.ShapeDtypeStruct((m, d), BF),
        compiler_params=_params(("parallel",)),
        name="rmsnorm_bf16",
    )(x, g.reshape(1, d))


def _dense_kernel(*refs, mode):
    o_ref = refs[-1]
    if mode == "plain":
        a, w = refs[:2]
        o_ref[...] = jnp.dot(a[...], w[...], preferred_element_type=F32).astype(o_ref.dtype)
    elif mode == "residual":
        a, w, r = refs[:3]
        o_ref[...] = r[...] + jnp.dot(a[...], w[...], preferred_element_type=F32)
    elif mode == "swiglu":
        a, wg, wu = refs[:3]
        g = jnp.dot(a[...], wg[...], preferred_element_type=F32)
        u = jnp.dot(a[...], wu[...], preferred_element_type=F32)
        o_ref[...] = (g * _sigmoid(g) * u).astype(o_ref.dtype)
    elif mode == "gated_sum":
        a, b, wa, wb, ga, gb = refs[:6]
        pa = jnp.dot(a[...], wa[...], preferred_element_type=F32)
        pb = jnp.dot(b[...], wb[...], preferred_element_type=F32)
        o_ref[...] = (_sigmoid(ga[...]) * pa + _sigmoid(gb[...]) * pb).astype(o_ref.dtype)
    else:
        raise ValueError(mode)


def _dense(mode, acts, weights, n, extras, extra_col_offsets, out_dtype, tm, name):
    m = acts[0].shape[0]
    tm = min(tm, m)
    in_specs = [pl.BlockSpec((tm, a.shape[1]), lambda i, j: (i, 0)) for a in acts]
    if weights[0].ndim == 3:
        tn = weights[0].shape[2]
        assert all(w.shape[0] == pl.cdiv(n, tn) and w.shape[2] == tn for w in weights)
        in_specs += [pl.BlockSpec((None, w.shape[1], tn), lambda i, j: (j, 0, 0)) for w in weights]
    else:
        tn = min(IN_TN, n)
        assert all(w.shape[1] == n for w in weights)
        in_specs += [pl.BlockSpec((w.shape[0], tn), lambda i, j: (0, j)) for w in weights]
    for off in extra_col_offsets:
        assert off % tn == 0
        in_specs.append(pl.BlockSpec((tm, tn), lambda i, j, o=off // tn: (i, j + o)))
    return pl.pallas_call(
        functools.partial(_dense_kernel, mode=mode),
        grid=(pl.cdiv(m, tm), pl.cdiv(n, tn)),
        in_specs=in_specs,
        out_specs=pl.BlockSpec((tm, tn), lambda i, j: (i, j)),
        out_shape=jax.ShapeDtypeStruct((m, n), out_dtype),
        compiler_params=_params(("parallel", "arbitrary")),
        name=name,
    )(*acts, *weights, *extras)


def _side_cast_specs(side, grid, tn):
    steps = math.prod(grid)
    strides = [math.prod(grid[a + 1:]) for a in range(len(grid))]
    in_specs, out_specs, shapes = [], [], []
    for x in side:
        rows, cols = x.shape
        r = BF16_ROWS * pl.cdiv(rows, BF16_ROWS * steps)
        while rows % r and r < rows:
            r += BF16_ROWS
        last = pl.cdiv(rows, r) - 1
        tw = min(tn, cols)
        nblk = pl.cdiv(cols, tw)

        def chunk(*idx, last=last):
            return jnp.minimum(sum(i * st for i, st in zip(idx[:len(grid)], strides)), last)

        in_specs.append(pl.BlockSpec((r, cols), lambda *idx, chunk=chunk: (chunk(*idx), 0)))
        out_specs.append(pl.BlockSpec((nblk, r, tw), lambda *idx, chunk=chunk: (0, chunk(*idx), 0)))
        shapes.append(jax.ShapeDtypeStruct((nblk, rows, tw), BF))
    return in_specs, out_specs, shapes


def _side_cast(side_in, side_out):
    for src, dst in zip(side_in, side_out):
        nblk, r, tw = dst.shape
        cols = src.shape[1]
        for j in range(nblk):
            w = min(tw, cols - j * tw)
            dst[j, :, :w] = src[:, j * tw:j * tw + w].astype(dst.dtype)
            if w < tw:
                dst[j, :, w:] = jnp.zeros((r, tw - w), dst.dtype)


def _split3(x):
    hi = x.astype(BF)
    r = x - hi.astype(F32)
    mid = r.astype(BF)
    lo = (r - mid.astype(F32)).astype(BF)
    return hi, mid, lo


def _gla_masks(r, c):
    row = lax.broadcasted_iota(jnp.int32, (r, r), 0)
    col = lax.broadcasted_iota(jnp.int32, (r, r), 1)
    rid = lax.broadcasted_iota(jnp.int32, (r, 1), 0)
    levels = []
    size = r
    while size > c:
        half = size // 2
        mask = jnp.logical_and(row // size == col // size,
                               jnp.logical_and(row % size >= half, col % size < half))
        levels.append((size, rid % size >= half, mask))
        size = half
    return row >= col, levels, jnp.logical_and(row >= col, row // c == col // c)


def _gla_front(qa, fr, ia, lb, masks, c, valid=None):
    r = qa.shape[0]
    tri, levels, chunk_mask = masks
    e = jnp.exp(-jnp.abs(fr))
    inv = 1.0 / (1.0 + e)
    pos = fr >= 0
    sig = jnp.where(pos, inv, e * inv)
    nsig = jnp.where(pos, e * inv, inv)
    logf = jnp.log(lb + (1.0 - lb) * sig)
    k = (1.0 - lb) * nsig
    if valid is not None:
        logf = jnp.where(valid, logf, 0.0)
        k = jnp.where(valid, k, 0.0)
    q = qa * _sigmoid(qa)
    v = ia.astype(BF)
    tri_b = tri.astype(BF)
    hi, mid, lo = _split3(logf)
    b = (jnp.dot(tri_b, hi, preferred_element_type=F32) + jnp.dot(tri_b, mid, preferred_element_type=F32)
         + jnp.dot(tri_b, lo, preferred_element_type=F32))

    def at_row(size, i):
        b3 = b.reshape(r // size, size, DH)
        return jnp.broadcast_to(b3[:, i:i + 1, :], b3.shape).reshape(r, DH)

    bm = at_row(c, c // 2 - 1)
    qh = (q * jnp.exp(jnp.minimum(b - bm, EXP_CLAMP))).astype(BF)
    kh = (k * jnp.exp(jnp.minimum(bm - b, EXP_CLAMP))).astype(BF)
    attn = jnp.where(chunk_mask, lax.dot_general(qh, kh, _NT, preferred_element_type=F32), 0.0)
    for size, upper, mask in levels:
        bs = at_row(size, size // 2 - 1)
        x = (jnp.where(upper, q, k) * jnp.exp(jnp.where(upper, b - bs, bs - b))).astype(BF)
        attn = attn + jnp.where(mask, lax.dot_general(x, x, _NT, preferred_element_type=F32), 0.0)
    return q, k, v, b, jnp.dot(attn.astype(BF), v, preferred_element_type=F32), at_row


def _gla_rows(st, qa, fr, ia, lb, masks, c):
    r = qa.shape[0]
    q, k, v, b, o, _ = _gla_front(qa, fr, ia, lb, masks, c)
    qs = (q * jnp.exp(b)).astype(BF)
    bl = b[r - 1:r, :]
    kt = (k * jnp.exp(bl - b)).astype(BF)
    o = o + lax.dot_general(qs, st.astype(BF), _NT, preferred_element_type=F32)
    st = st * jnp.exp(bl) + lax.dot_general(v, kt, _TN, preferred_element_type=F32)
    return st, o


def _gla_out(o, og, gn):
    return (_row_rmsnorm(o, gn) * (og * _sigmoid(og))).astype(BF)


def _gla_prompt_kernel(qa_ref, fa_ref, ia_ref, og_ref, mfa_ref, mia_ref, lb_ref, gn_ref,
                       oa_ref, s_ref, *, n_iters):
    lb = lb_ref[...]
    gn = gn_ref[...]
    st, _ = _gla_rows(jnp.zeros((DH, DH), F32), jnp.zeros((N_META, DH), F32), mfa_ref[...], mia_ref[...], lb,
                      _gla_masks(N_META, GLA_CHUNK), GLA_CHUNK)
    masks = _gla_masks(GLA_ROWS, GLA_CHUNK)

    def body(i, st):
        rows = pl.ds(pl.multiple_of(i * GLA_ROWS, GLA_ROWS), GLA_ROWS)
        st, o = _gla_rows(st, qa_ref[rows, :], fa_ref[rows, :], ia_ref[rows, :], lb, masks, GLA_CHUNK)
        oa_ref[rows, :] = _gla_out(o, og_ref[rows, :], gn)
        return st

    st = lax.fori_loop(0, n_iters, body, st, unroll=GLA_UNROLL if n_iters % GLA_UNROLL == 0 else 1)
    s_ref[...] = st.T


def _gla_prompt(proj, proj_meta, meta_blk, lb, g_norm, nb, t):
    blk = lambda cb: pl.BlockSpec((t, DH), lambda n, h, cb=cb: (n, cb * HEADS + h))
    mblk = lambda cb: pl.BlockSpec((N_META, DH), lambda n, h, cb=cb: (meta_blk, cb * HEADS + h))
    return pl.pallas_call(
        functools.partial(_gla_prompt_kernel, n_iters=t // GLA_ROWS),
        grid=(nb, HEADS),
        in_specs=[blk(COL_QA), blk(COL_FA), blk(COL_IA), blk(COL_OG), mblk(COL_FA), mblk(COL_IA),
                  pl.BlockSpec((1, DH), lambda n, h: (0, h)), pl.BlockSpec((1, DH), lambda n, h: (0, 0))],
        out_specs=[pl.BlockSpec((t, DH), lambda n, h: (n, h)),
                   pl.BlockSpec((None, None, DH, DH), lambda n, h: (n, h, 0, 0))],
        out_shape=[jax.ShapeDtypeStruct((nb * t, HW), BF),
                   jax.ShapeDtypeStruct((nb, HEADS, DH, DH), F32)],
        compiler_params=_params(("parallel", "parallel")),
        name="gla_prompt",
    )(proj, proj, proj, proj, proj_meta, proj_meta, lb.reshape(1, HW), g_norm.reshape(1, DH))


def _gla_sample_kernel(qa_ref, fa_ref, ia_ref, og_ref, s0_ref, lb_ref, gn_ref, oa_ref, s_ref, *, tn):
    c = SUBLANES
    r = HEADS * c
    row = lax.broadcasted_iota(jnp.int32, (r, r), 0)
    col = lax.broadcasted_iota(jnp.int32, (r, r), 1)
    own = jnp.logical_and(row >= col, row // c == col // c)
    valid = lax.broadcasted_iota(jnp.int32, (r, DH), 0) % c < tn
    pad = jnp.zeros((c - tn, DH), F32)

    def stack(ref):
        return jnp.concatenate([x for h in range(HEADS) for x in (ref[:, h * DH:(h + 1) * DH], pad)], axis=0)

    lb = jnp.concatenate([jnp.broadcast_to(lb_ref[:, h * DH:(h + 1) * DH], (c, DH)) for h in range(HEADS)], axis=0)
    q, k, v, b, o, at_row = _gla_front(stack(qa_ref), stack(fa_ref), stack(ia_ref), lb, (own, [], own), c, valid)
    qs = (q * jnp.exp(b)).astype(BF)
    bl = at_row(c, c - 1)
    kt = (k * jnp.exp(bl - b)).astype(BF)
    decay = jnp.exp(bl)
    o_state = []
    for h in range(HEADS):
        rows = slice(h * c, (h + 1) * c)
        st = s0_ref[h]
        o_state.append(jnp.dot(qs[rows], st.astype(BF), preferred_element_type=F32))
        decay_col = decay[rows].T[:, 0:1]
        s_ref[h] = st * decay_col + lax.dot_general(kt[rows], v[rows], _TN, preferred_element_type=F32)
    out = _gla_out(o + jnp.concatenate(o_state, axis=0), stack(og_ref), gn_ref[...])
    for h in range(HEADS):
        oa_ref[:, h * DH:(h + 1) * DH] = out[h * c:h * c + tn, :]


def _gla_sample(proj3, state, lb, g_norm):
    db, tn = state.shape[0], proj3.shape[1]
    assert tn <= SUBLANES
    blk = lambda cb: pl.BlockSpec((None, tn, HW), lambda b, cb=cb: (b, 0, cb))
    sblk = pl.BlockSpec((None, HEADS, DH, DH), lambda b: (b, 0, 0, 0))
    return pl.pallas_call(
        functools.partial(_gla_sample_kernel, tn=tn),
        grid=(db,),
        in_specs=[blk(COL_QA), blk(COL_FA), blk(COL_IA), blk(COL_OG), sblk,
                  pl.BlockSpec((1, HW), lambda b: (0, 0)), pl.BlockSpec((1, DH), lambda b: (0, 0))],
        out_specs=[pl.BlockSpec((None, tn, HW), lambda b: (b, 0, 0)), sblk],
        out_shape=[jax.ShapeDtypeStruct((db, tn, HW), BF),
                   jax.ShapeDtypeStruct((db, HEADS, DH, DH), F32)],
        compiler_params=_params(("parallel",)),
        name="gla_sample",
    )(proj3, proj3, proj3, proj3, state, lb.reshape(1, HW), g_norm.reshape(1, DH))


def _sb_tail(z, w2, mask):
    kk = z.shape[1]
    log_beta = jnp.minimum(z, 0.0) - jnp.log(1.0 + jnp.exp(-jnp.abs(z)))
    l1m = log_beta - z
    if mask is not None:
        l1m = jnp.where(mask, l1m, 0.0)
    hi = l1m.astype(BF)
    lo = (l1m - hi.astype(F32)).astype(BF)
    r = jnp.dot(jnp.concatenate([hi, lo], axis=1), w2, preferred_element_type=F32)
    return log_beta + r[:, :kk], r[:, kk:]


def _sb_exp(logw, c, mask):
    a = jnp.exp(logw + c)
    return a if mask is None else jnp.where(mask, a, 0.0)


def _sb_w2(kk):
    row = lax.broadcasted_iota(jnp.int32, (2 * kk, 2 * kk), 0) % kk
    col = lax.broadcasted_iota(jnp.int32, (2 * kk, 2 * kk), 1)
    return jnp.logical_or(col >= kk, row > col).astype(BF)


def _sb_prompt_kernel(qb_ref, kb_ref, vb_ref, mk_ref, mv_ref, gq_ref, gk_ref, bias_ref, *refs, qtile, n_side):
    side_in, (ob_ref, ko_ref, vo_ref), side_out = refs[:n_side], refs[n_side:n_side + 3], refs[n_side + 3:2 * n_side + 3]
    q_s, k_s, v_s, mk_s, mv_s, w_s, c_s, acc_s = refs[2 * n_side + 3:]
    _side_cast(side_in, side_out)
    it = pl.program_id(2)
    blk = SB_BLOCK
    nsub = qtile // blk
    scale = 1.0 / math.sqrt(DH)

    @pl.when(it == 0)
    def _():
        kn = _row_rmsnorm(kb_ref[...], gk_ref[...])
        ko_ref[N_META:, :] = kn
        k_s[...] = kn.astype(BF)
        mkn = _row_rmsnorm(mk_ref[...], gk_ref[...])
        ko_ref[:N_META, :] = mkn
        mk_s[...] = jnp.zeros(mk_s.shape, BF)
        mk_s[:N_META, :] = mkn.astype(BF)
        v = vb_ref[...]
        vo_ref[N_META:, :] = v
        v_s[...] = v.astype(BF)
        mv = mv_ref[...]
        vo_ref[:N_META, :] = mv
        mv_s[...] = jnp.zeros(mv_s.shape, BF)
        mv_s[:N_META, :] = mv.astype(BF)
        q_s[...] = _row_rmsnorm(qb_ref[...], gq_ref[...]).astype(BF)
        w_s[...] = _sb_w2(blk)

    bias = bias_ref[...]
    w2 = w_s[...]
    c_s[...] = jnp.zeros(c_s.shape, F32)
    acc_s[...] = jnp.zeros(acc_s.shape, F32)

    def sweep(r0, kj, vj, visible):
        nr = qtile - r0
        rows = pl.ds(r0, nr)
        qi = q_s[pl.ds(pl.multiple_of(it * qtile, qtile) + r0, nr), :]
        z = lax.dot_general(qi, kj, _NT, preferred_element_type=F32) * scale + bias
        col = lax.broadcasted_iota(jnp.int32, (nr, blk), 1)
        if visible is None:
            m = None
        elif visible == "causal":
            m = col < lax.broadcasted_iota(jnp.int32, (nr, blk), 0)
        else:
            m = col < visible
        logw, tot = _sb_tail(z, w2, m)
        c = c_s[rows, :]
        a = _sb_exp(logw, c, m)
        c_s[rows, :] = c + tot
        acc_s[rows, :] += jnp.dot(a.astype(BF), vj, preferred_element_type=F32)

    def key_rows(j):
        return pl.ds(pl.multiple_of(j * blk, blk), blk)

    for s in reversed(range(nsub)):
        j = it * nsub + s
        sweep(s * blk, k_s[key_rows(j), :], v_s[key_rows(j), :], "causal")

    def body(jj, carry):
        for u in range(nsub):
            j = (it - jj) * nsub - 1 - u
            sweep(0, k_s[key_rows(j), :], v_s[key_rows(j), :], None)
        return carry

    lax.fori_loop(0, it, body, 0)
    sweep(0, mk_s[...], mv_s[...], N_META)
    ob_ref[...] = acc_s[...].astype(ob_ref.dtype)


def _sb_prompt(proj, proj_meta, meta_blk, gq, gk, bias, nb, t, side):
    qtile = min(SB_QTILE, t)
    assert t % qtile == 0 and qtile % SB_BLOCK == 0
    nq = t // qtile
    side_in_specs, side_out_specs, side_shapes = _side_cast_specs(side, (nb, HEADS, nq), TAIL_TN)
    blk = lambda cb: pl.BlockSpec((t, DH), lambda n, h, i, cb=cb: (n, cb * HEADS + h))
    mblk = lambda cb: pl.BlockSpec((N_META, DH), lambda n, h, i, cb=cb: (meta_blk, cb * HEADS + h))
    vec = pl.BlockSpec((1, DH), lambda n, h, i: (0, 0))
    kv_out = pl.BlockSpec((None, N_META + t, DH), lambda n, h, i: (n, 0, h))
    bias_b = jnp.broadcast_to(bias.astype(F32).reshape(HEADS, 1, 1), (HEADS, 1, SB_BLOCK))
    out = pl.pallas_call(
        functools.partial(_sb_prompt_kernel, qtile=qtile, n_side=len(side)),
        grid=(nb, HEADS, nq),
        in_specs=[blk(COL_QB), blk(COL_KB), blk(COL_VB), mblk(COL_KB), mblk(COL_VB), vec, vec,
                  pl.BlockSpec((None, 1, SB_BLOCK), lambda n, h, i: (h, 0, 0))] + side_in_specs,
        out_specs=[pl.BlockSpec((qtile, DH), lambda n, h, i: (n * nq + i, h)), kv_out, kv_out] + side_out_specs,
        out_shape=[jax.ShapeDtypeStruct((nb * t, HW), BF),
                   jax.ShapeDtypeStruct((nb, N_META + t, HW), F32),
                   jax.ShapeDtypeStruct((nb, N_META + t, HW), F32)] + side_shapes,
        scratch_shapes=[pltpu.VMEM((t, DH), BF), pltpu.VMEM((t, DH), BF), pltpu.VMEM((t, DH), BF),
                        pltpu.VMEM((SB_BLOCK, DH), BF), pltpu.VMEM((SB_BLOCK, DH), BF),
                        pltpu.VMEM((2 * SB_BLOCK, 2 * SB_BLOCK), BF),
                        pltpu.VMEM((qtile, SB_BLOCK), F32), pltpu.VMEM((qtile, DH), F32)],
        compiler_params=_params(("arbitrary", "arbitrary", "arbitrary")),
        name="sb_prompt",
    )(proj, proj, proj, proj_meta, proj_meta, gq.reshape(1, DH), gk.reshape(1, DH), bias_b, *side)
    return out[0], out[1], out[2], out[3:]


def _sb_sample_kernel(pt_ref, qb_ref, kb_ref, vb_ref, *refs, tn, n_steps, pages_per_step):
    page_refs = refs[:4 * pages_per_step]
    gq_ref, gk_ref, bias_ref, ob_ref, ko_ref, vo_ref, q_s, c_s, acc_s, w_s = refs[4 * pages_per_step:]
    p = pl.program_id(1)
    rq = 16
    ro = SUBLANES
    half = HEADS // 2
    pg = page_refs[0].shape[0]
    scale = 1.0 / math.sqrt(DH)
    zrows = jnp.zeros((rq - ro, pg), F32)

    def sweep(blocks, mask):
        nblk = len(blocks)
        hr = HEADS * ro
        z = jnp.concatenate(
            [jnp.dot(q_s[h * rq:(h + 1) * rq, :], get_k(h).T.astype(BF), preferred_element_type=F32)[:ro]
             for get_k, _ in blocks for h in range(HEADS)], axis=0)
        z = (z.reshape(nblk, hr, pg) * scale + bias_ref[...]).reshape(nblk * hr, pg)
        c0 = c_s[...]
        logw, tot = _sb_tail(z, w_s[...], mask)
        cs = []
        for s in range(nblk):
            cs.append(c0)
            c0 = c0 + tot[s * hr:(s + 1) * hr]
        c_s[...] = c0
        a = _sb_exp(logw, jnp.concatenate(cs, axis=0), mask)
        acc = acc_s[...]
        for s, (_, get_v) in enumerate(blocks):
            acc = acc + jnp.concatenate(
                [jnp.dot(jnp.concatenate([a[s * hr + h * ro:s * hr + (h + 1) * ro], zrows], axis=0).astype(BF),
                         get_v(h).astype(BF), preferred_element_type=F32)[:ro] for h in range(HEADS)], axis=0)
        acc_s[...] = acc

    @pl.when(p == 0)
    def _():
        w_s[...] = _sb_w2(pg)
        c_s[...] = jnp.zeros(c_s.shape, F32)
        acc_s[...] = jnp.zeros(acc_s.shape, F32)
        q_s[...] = jnp.zeros(q_s.shape, BF)
        vo_ref[...] = vb_ref[...]
        kpad = jnp.zeros((pg - tn, DH), F32)
        ks, vs = [], []
        for h in range(HEADS):
            cols = slice(h * DH, (h + 1) * DH)
            q_s[h * rq:h * rq + tn, :] = _row_rmsnorm(qb_ref[:, cols], gq_ref[...]).astype(BF)
            kn = _row_rmsnorm(kb_ref[:, cols], gk_ref[...])
            ko_ref[:, cols] = kn
            ks.append(jnp.concatenate([kn, kpad], axis=0))
            vs.append(jnp.concatenate([vb_ref[:, cols], kpad], axis=0))
        row = lax.broadcasted_iota(jnp.int32, (HEADS * ro, pg), 0) % ro
        col = lax.broadcasted_iota(jnp.int32, (HEADS * ro, pg), 1)
        sweep([(lambda h: ks[h], lambda h: vs[h])], jnp.logical_and(col < row, col < tn))

    def head(lo, hi, h):
        return (lo if h < half else hi)[pl.ds(h % half, pg, stride=half), :]

    blocks = []
    for s in range(pages_per_step):
        k_lo, k_hi, v_lo, v_hi = [r.reshape(pg * half, DH) for r in page_refs[4 * s:4 * s + 4]]
        blocks.append((functools.partial(head, k_lo, k_hi), functools.partial(head, v_lo, v_hi)))
    sweep(blocks, None)

    @pl.when(p == n_steps - 1)
    def _():
        for h in range(HEADS):
            ob_ref[:, h * DH:(h + 1) * DH] = acc_s[h * ro:h * ro + tn, :].astype(ob_ref.dtype)


def _sb_sample(proj3, cache_k, cache_v, page_table, gq, gk, bias):
    db, tn = page_table.shape[0], proj3.shape[1]
    n_pages = page_table.shape[1]
    pg = cache_k.shape[1]
    pps = SB_PAGES_PER_STEP if n_pages % SB_PAGES_PER_STEP == 0 else 1
    n_steps = n_pages // pps
    half = HEADS // 2
    assert tn <= SUBLANES and half == SUBLANES
    blk = lambda cb: pl.BlockSpec((None, tn, HW), lambda b, p, pt, cb=cb: (b, 0, cb))

    def page(s, g):
        return pl.BlockSpec((None, pg, half, DH),
                            lambda b, p, pt, s=s, g=g: (pt[b, n_pages - 1 - (p * pps + s)], 0, g, 0))

    pages, page_specs = [], []
    for s in range(pps):
        pages += [cache_k, cache_k, cache_v, cache_v]
        page_specs += [page(s, 0), page(s, 1), page(s, 0), page(s, 1)]
    vec = pl.BlockSpec((1, DH), lambda b, p, pt: (0, 0))
    out = pl.BlockSpec((None, tn, HW), lambda b, p, pt: (b, 0, 0))
    bias_b = jnp.broadcast_to(bias.astype(F32).reshape(HEADS, 1, 1), (HEADS, SUBLANES, pg)).reshape(HEADS * SUBLANES, pg)
    return pl.pallas_call(
        functools.partial(_sb_sample_kernel, tn=tn, n_steps=n_steps, pages_per_step=pps),
        grid_spec=pltpu.PrefetchScalarGridSpec(
            num_scalar_prefetch=1,
            grid=(db, n_steps),
            in_specs=[blk(COL_QB), blk(COL_KB), blk(COL_VB)] + page_specs + [
                vec, vec, pl.BlockSpec((HEADS * SUBLANES, pg), lambda b, p, pt: (0, 0))],
            out_specs=[out, out, out],
            scratch_shapes=[pltpu.VMEM((HEADS * 16, DH), BF), pltpu.VMEM((HEADS * SUBLANES, pg), F32),
                            pltpu.VMEM((HEADS * SUBLANES, DH), F32), pltpu.VMEM((2 * pg, 2 * pg), BF)]),
        out_shape=[jax.ShapeDtypeStruct((db, tn, HW), BF),
                   jax.ShapeDtypeStruct((db, tn, HW), F32),
                   jax.ShapeDtypeStruct((db, tn, HW), F32)],
        compiler_params=_params(("parallel", "arbitrary")),
        name="sb_sample",
    )(page_table, proj3, proj3, proj3, *pages, gq.reshape(1, DH), gk.reshape(1, DH), bias_b)


def _finish(x, oa, ob, proj, w_pa, w_pb, w_o, g_ffn, w_gate, w_up, w_down, d_ff, tm):
    d = x.shape[1]
    merged = _dense("gated_sum", [oa, ob], [w_pa, w_pb], d, [proj, proj], [COL_GTA, COL_GTA + d], BF, tm,
                    "merge_proj")
    x1 = _dense("residual", [merged], [w_o], d, [x], [0], F32, tm, "out_proj")
    h = _rmsnorm_bf16(x1, g_ffn)
    act = _dense("swiglu", [h], [w_gate, w_up], d_ff, [], [], BF, tm, "ffn_up")
    return _dense("residual", [act], [w_down], d, [x1], [0], F32, min(tm, 512), "ffn_down")


def kernel(x_prompt, x_sample, cache_k, cache_v, state_hgrn, page_table, meta_tokens, g_mix, w_in,
           hgrn_lb_logits, hgrn_g_norm, sb_gq, sb_gk, sb_bias, w_pa, w_pb, w_o, g_ffn, w_gate, w_up, w_down):
    nb, t, d = x_prompt.shape
    db, tn, _ = x_sample.shape
    depth = w_in.shape[0]
    assert depth == 1, "single-layer trunk"
    assert t % GLA_ROWS == 0 and N_META % GLA_CHUNK == 0
    l = 0
    lb = jnp.cumsum(jax.nn.softmax(hgrn_lb_logits.astype(F32), axis=0), axis=0)[l]
    lead = lambda a: a.reshape(a.shape[1:])
    in_width = w_in.shape[-1]
    w_in_b = lead(w_in).astype(BF)

    xp = x_prompt.reshape(nb * t, d)
    xs = x_sample.reshape(db * tn, d)

    def in_proj(x, tm):
        return _dense("plain", [_rmsnorm_bf16(x, g_mix[l])], [w_in_b], in_width, [], [], F32, tm, "in_proj")

    ns = db * tn
    assert ns % N_META == 0, "decode rows must fill whole meta-sized row blocks"
    meta_blk = ns // N_META
    proj_s = in_proj(jnp.concatenate([xs, meta_tokens.astype(xs.dtype)], axis=0), ns + N_META)
    proj_p = in_proj(xp, 1024)
    assert N_META % tn == 0
    proj_s3 = proj_s.reshape(db + N_META // tn, tn, -1)

    oa_p, s_p = _gla_prompt(proj_p, proj_s, meta_blk, lb, hgrn_g_norm[l], nb, t)
    ob_p, k_p, v_p, (w_pa_b, w_pb_b, w_o_b, w_gate_b, w_up_b, w_down_b) = _sb_prompt(
        proj_p, proj_s, meta_blk, sb_gq[l], sb_gk[l], sb_bias[l], nb, t,
        [lead(w) for w in (w_pa, w_pb, w_o, w_gate, w_up, w_down)])
    lw = (w_pa_b, w_pb_b, w_o_b, g_ffn[l], w_gate_b, w_up_b, w_down_b)
    d_ff = w_gate.shape[-1]
    y_p = _finish(xp, oa_p, ob_p, proj_p, *lw, d_ff, tm=1024)

    oa_s, s_s = _gla_sample(proj_s3, lead(state_hgrn), lb, hgrn_g_norm[l])
    ob_s, k_s, v_s = _sb_sample(proj_s3, lead(cache_k), lead(cache_v), page_table, sb_gq[l], sb_gk[l],
                                sb_bias[l])
    y_s = _finish(xs, oa_s.reshape(db * tn, HW), ob_s.reshape(db * tn, HW), proj_s, *lw, d_ff, tm=512)

    return (y_p.reshape(nb, t, d), y_s.reshape(db, tn, d),
            k_p.reshape(1, nb, N_META + t, HEADS, DH), v_p.reshape(1, nb, N_META + t, HEADS, DH),
            s_p[None],
            k_s.reshape(1, db, tn, HEADS, DH).astype(cache_k.dtype),
            v_s.reshape(1, db, tn, HEADS, DH).astype(cache_v.dtype),
            s_s[None].astype(state_hgrn.dtype))
```
